```python
import jax
import jax.numpy as jnp
from jax import lax
import numpy as np

D_MODEL = 2048
BATCH = 1
SEQ = 16384
DEPTH = 1
DEC_BATCH = 16
DEC_SEQ = 2048
PAST_LEN = 128

GRID_W = 64
N_MEM = 256
GLA_HEADS = 4
GLA_DK = 128
GLA_DV = 256
GLA_KEY = GLA_HEADS * GLA_DK
GLA_VAL = GLA_HEADS * GLA_DV
GLA_RANK = 16
GLA_TAU = 16.0
GLA_CHUNK = 64
GQA_HEADS = 8
GQA_KV_HEADS = 2
HEAD_DIM = 128
GQA_Q = GQA_HEADS * HEAD_DIM
GQA_KV = GQA_KV_HEADS * HEAD_DIM
Q_BLOCK = 128
ROPE_THETA = 10000.0
MEM_HEADS = 4
MEM_HD = 256
MEM_W = MEM_HEADS * MEM_HD
N_BRANCH = 3
N_GROUPS = 4
EXPERTS_PER_GROUP = 8
N_EXPERTS = N_GROUPS * EXPERTS_PER_GROUP
TOP_K = 2
D_EXPERT = 512
MOE_BLOCK = 128
EPS = 1e-6
IN_SIZES = (GLA_KEY, GLA_KEY, GLA_VAL, GLA_VAL, GLA_RANK, GLA_RANK, GQA_Q, GQA_KV, GQA_KV, MEM_W, N_BRANCH * D_MODEL)
IN_COLS = 2 * GLA_KEY + 2 * GLA_VAL + 2 * GLA_RANK + GQA_Q + 2 * GQA_KV + MEM_W + N_BRANCH * D_MODEL

kernel_name = 'hybrid_gla_axialgqa_mem_hmoe_encoder'


def rms_norm(x, g):
    xf = x.astype(jnp.float32)
    y = xf * lax.rsqrt(jnp.mean(xf * xf, axis=-1, keepdims=True) + EPS)
    return (y * g.astype(jnp.float32)).astype(x.dtype)


def axial_rope(L):
    rows = L // GRID_W
    row = jnp.broadcast_to(jnp.arange(rows, dtype=jnp.float32)[:, None], (rows, GRID_W)).reshape(L)
    col = jnp.broadcast_to(jnp.arange(GRID_W, dtype=jnp.float32)[None, :], (rows, GRID_W)).reshape(L)
    axis_dim = HEAD_DIM // 2
    freqs = ROPE_THETA ** (-jnp.arange(0, axis_dim, 2, dtype=jnp.float32) / axis_dim)
    ang = jnp.concatenate([row[:, None] * freqs, col[:, None] * freqs], axis=-1)
    return jnp.cos(ang), jnp.sin(ang)


def apply_rope(x, cos, sin):
    xf = x.astype(jnp.float32).reshape(x.shape[:-1] + (HEAD_DIM // 2, 2))
    x0 = xf[..., 0]
    x1 = xf[..., 1]
    c = cos[None, :, None, :]
    s = sin[None, :, None, :]
    out = jnp.stack([x0 * c - x1 * s, x0 * s + x1 * c], axis=-1)
    return out.reshape(x.shape).astype(x.dtype)


def gla_chunk_scan(q, k, v, log_a, include_diag):
    B, H, L, dk = q.shape
    dv = v.shape[-1]
    C = GLA_CHUNK
    N = L // C
    q = q.reshape(B, H, N, C, dk)
    k = k.reshape(B, H, N, C, dk)
    v = v.reshape(B, H, N, C, dv)
    b = jnp.cumsum(log_a.reshape(B, H, N, C, dk), axis=3)
    q_e = q * jnp.exp(b)
    k_e = k * jnp.exp(-b)
    mask = jnp.tril(jnp.ones((C, C), dtype=bool), 0 if include_diag else -1)
    att = jnp.where(mask, jnp.einsum('bhncd,bhnsd->bhncs', q_e, k_e), 0.0)
    o_intra = jnp.einsum('bhncs,bhnsv->bhncv', att, v)
    b_end = b[:, :, :, -1:, :]
    kv_chunk = jnp.einsum('bhncd,bhncv->nbhdv', k * jnp.exp(b_end - b), v)
    decay = jnp.moveaxis(jnp.exp(b_end[:, :, :, 0, :]), 2, 0)

    def step(S, inp):
        d, kv = inp
        return d[..., None] * S + kv, S

    _, S_prev = lax.scan(step, jnp.zeros((B, H, dk, dv), jnp.float32), (decay, kv_chunk))
    o_inter = jnp.einsum('bhncd,nbhdv->bhncv', q_e, S_prev)
    return (o_intra + o_inter).reshape(B, H, L, dv)


def gqa_attention(q, k, v):
    B, L = q.shape[0], q.shape[1]
    G = GQA_HEADS // GQA_KV_HEADS
    nb = L // Q_BLOCK
    qb = q.reshape(B, nb, Q_BLOCK, GQA_KV_HEADS, G, HEAD_DIM).transpose(1, 0, 2, 3, 4, 5)
    scale = HEAD_DIM ** -0.5

    def one_block(qi):
        s = jnp.einsum('bqkgd,bskd->bkgqs', qi, k, preferred_element_type=jnp.float32) * scale
        p = jax.nn.softmax(s, axis=-1).astype(v.dtype)
        return jnp.einsum('bkgqs,bskd->bqkgd', p, v)

    ob = lax.map(one_block, qb)
    return ob.transpose(1, 0, 2, 3, 4, 5).reshape(B, L, GQA_Q)


def mem_attention(q, k, v):
    B, L = q.shape[0], q.shape[1]
    s = jnp.einsum('blhd,bmhd->bhlm', q, k, preferred_element_type=jnp.float32) * (MEM_HD ** -0.5)
    p = jax.nn.softmax(s, axis=-1).astype(v.dtype)
    return jnp.einsum('bhlm,bmhd->blhd', p, v).reshape(B, L, MEM_W)


def hier_moe(h, w_router_group, b_router_group, w_router_expert, b_router_expert, w_exp_gate, w_exp_up, w_exp_down):
    B, L, D = h.shape
    T = B * L
    xt = h.reshape(T, D)
    hf = xt.astype(jnp.float32)
    tok = jnp.arange(T)
    grp_logits = hf @ w_router_group.astype(jnp.float32) + b_router_group.astype(jnp.float32)
    grp_prob = jax.nn.softmax(grp_logits, axis=-1)
    grp = jnp.argmax(grp_logits, axis=-1).astype(jnp.int32)
    p_grp = grp_prob[tok, grp]
    exp_logits = (hf @ w_router_expert.astype(jnp.float32) + b_router_expert.astype(jnp.float32))
    exp_logits = exp_logits.reshape(T, N_GROUPS, EXPERTS_PER_GROUP)[tok, grp]
    top_p, top_i = lax.top_k(jax.nn.softmax(exp_logits, axis=-1), TOP_K)
    weights = p_grp[:, None] * top_p / jnp.sum(top_p, axis=-1, keepdims=True)
    expert_ids = grp[:, None] * EXPERTS_PER_GROUP + top_i.astype(jnp.int32)

    A = T * TOP_K
    e_flat = expert_ids.reshape(A)
    w_flat = weights.reshape(A)
    tok_flat = jnp.arange(A, dtype=jnp.int32) // TOP_K
    order = jnp.argsort(e_flat)
    e_s = e_flat[order]
    tok_s = tok_flat[order]
    w_s = w_flat[order]
    counts = jnp.zeros((N_EXPERTS,), jnp.int32).at[e_flat].add(1)
    starts = jnp.cumsum(counts) - counts
    padded = ((counts + MOE_BLOCK - 1) // MOE_BLOCK) * MOE_BLOCK
    pad_ends = jnp.cumsum(padded)
    pad_starts = pad_ends - padded
    dest = pad_starts[e_s] + (jnp.arange(A, dtype=jnp.int32) - starts[e_s])
    n_blk = A // MOE_BLOCK + N_EXPERTS
    P = n_blk * MOE_BLOCK
    slot_tok = jnp.full((P,), T, jnp.int32).at[dest].set(tok_s)
    slot_w = jnp.zeros((P,), jnp.float32).at[dest].set(w_s)
    blk_expert = jnp.clip(jnp.searchsorted(pad_ends, jnp.arange(n_blk, dtype=jnp.int32) * MOE_BLOCK, side='right'), 0, N_EXPERTS - 1)
    x_pad = jnp.concatenate([xt, jnp.zeros((1, D), xt.dtype)], axis=0)

    def run_block(args):
        ids, wts, e = args
        xb = x_pad[ids]
        hid = jax.nn.silu(xb @ w_exp_gate[e]) * (xb @ w_exp_up[e])
        return (hid @ w_exp_down[e]) * wts[:, None].astype(xb.dtype)

    yb = lax.map(run_block, (slot_tok.reshape(n_blk, MOE_BLOCK), slot_w.reshape(n_blk, MOE_BLOCK), blk_expert))
    y = jnp.zeros((T + 1, D), yb.dtype).at[slot_tok].add(yb.reshape(P, D))[:T]
    return y.reshape(B, L, D).astype(h.dtype)


def hybrid_layer(x, mem, g_mix, w_in, gla_a2_fwd, gla_ab_fwd, gla_a2_bwd, gla_ab_bwd, g_gla_out,
                 g_q_gqa, g_k_gqa, g_mem_norm, w_mem_kv, g_q_mem, g_k_mem,
                 w_br_gla, w_br_gqa, w_br_mem, w_out, g_ffn,
                 w_router_group, b_router_group, w_router_expert, b_router_expert,
                 w_exp_gate, w_exp_up, w_exp_down):
    B, L, D = x.shape
    h = rms_norm(x, g_mix)
    proj = h @ w_in
    split_points = np.cumsum(IN_SIZES)[:-1].tolist()
    gq, gk, gv, gg, ga_f, ga_b, aq, ak, av, mq, mg = jnp.split(proj, split_points, axis=-1)

    def heads(t, n, d):
        return t.reshape(B, L, n, d).transpose(0, 2, 1, 3).astype(jnp.float32)

    q_gla = heads(gq, GLA_HEADS, GLA_DK) * (GLA_DK ** -0.5)
    k_gla = heads(gk, GLA_HEADS, GLA_DK)
    v_gla = heads(gv, GLA_HEADS, GLA_DV)
    la_f = heads(jax.nn.log_sigmoid((ga_f @ gla_a2_fwd).astype(jnp.float32) + gla_ab_fwd.astype(jnp.float32)) / GLA_TAU, GLA_HEADS, GLA_DK)
    la_b = heads(jax.nn.log_sigmoid((ga_b @ gla_a2_bwd).astype(jnp.float32) + gla_ab_bwd.astype(jnp.float32)) / GLA_TAU, GLA_HEADS, GLA_DK)
    o_f = gla_chunk_scan(q_gla, k_gla, v_gla, la_f, True)
    o_b = jnp.flip(gla_chunk_scan(jnp.flip(q_gla, 2), jnp.flip(k_gla, 2), jnp.flip(v_gla, 2), jnp.flip(la_b, 2), False), 2)
    o_gla = rms_norm((o_f + o_b).transpose(0, 2, 1, 3), g_gla_out).reshape(B, L, GLA_VAL).astype(x.dtype)
    branch_gla = (o_gla * jax.nn.silu(gg)) @ w_br_gla

    cos, sin = axial_rope(L)
    qa = apply_rope(rms_norm(aq.reshape(B, L, GQA_HEADS, HEAD_DIM), g_q_gqa), cos, sin)
    ka = apply_rope(rms_norm(ak.reshape(B, L, GQA_KV_HEADS, HEAD_DIM), g_k_gqa), cos, sin)
    va = av.reshape(B, L, GQA_KV_HEADS, HEAD_DIM)
    branch_gqa = gqa_attention(qa, ka, va) @ w_br_gqa

    M = mem.shape[1]
    mk, mv = jnp.split(rms_norm(mem, g_mem_norm) @ w_mem_kv, 2, axis=-1)
    qm = rms_norm(mq.reshape(B, L, MEM_HEADS, MEM_HD), g_q_mem)
    km = rms_norm(mk.reshape(B, M, MEM_HEADS, MEM_HD), g_k_mem)
    vm = mv.reshape(B, M, MEM_HEADS, MEM_HD)
    branch_mem = mem_attention(qm, km, vm) @ w_br_mem

    gates = jax.nn.sigmoid(mg.astype(jnp.float32)).reshape(B, L, N_BRANCH, D).astype(x.dtype)
    merged = gates[:, :, 0] * branch_gla + gates[:, :, 1] * branch_gqa + gates[:, :, 2] * branch_mem
    x = x + merged @ w_out

    x = x + hier_moe(rms_norm(x, g_ffn), w_router_group, b_router_group, w_router_expert, b_router_expert,
                     w_exp_gate, w_exp_up, w_exp_down)
    return x


def setup_inputs(seed: int = 0) -> dict:
    key = jax.random.key(seed)
    ks = jax.random.split(key, 32)

    def nrm(k, shape, scale):
        return jax.random.normal(k, shape, jnp.float32) * scale

    def gain(k, shape):
        return 1.0 + 0.05 * jax.random.normal(k, shape, jnp.float32)

    return {
        'x_prompt': nrm(ks[0], (BATCH, SEQ, D_MODEL), 1.0),
        'x_sample': nrm(ks[1], (DEC_BATCH, DEC_SEQ, D_MODEL), 1.0),
        'mem_prompt': nrm(ks[2], (BATCH, N_MEM, D_MODEL), 1.0),
        'mem_sample': nrm(ks[3], (DEC_BATCH, N_MEM, D_MODEL), 1.0),
        'g_mix': gain(ks[4], (DEPTH, D_MODEL)),
        'w_in': nrm(ks[5], (DEPTH, D_MODEL, IN_COLS), D_MODEL ** -0.5),
        'gla_a2_fwd': nrm(ks[6], (DEPTH, GLA_RANK, GLA_KEY), GLA_RANK ** -0.5),
        'gla_ab_fwd': nrm(ks[7], (DEPTH, GLA_KEY), 0.1),
        'gla_a2_bwd': nrm(ks[8], (DEPTH, GLA_RANK, GLA_KEY), GLA_RANK ** -0.5),
        'gla_ab_bwd': nrm(ks[9], (DEPTH, GLA_KEY), 0.1),
        'g_gla_out': gain(ks[10], (DEPTH, GLA_DV)),
        'g_q_gqa': gain(ks[11], (DEPTH, HEAD_DIM)),
        'g_k_gqa': gain(ks[12], (DEPTH, HEAD_DIM)),
        'g_mem_norm': gain(ks[13], (DEPTH, D_MODEL)),
        'w_mem_kv': nrm(ks[14], (DEPTH, D_MODEL, 2 * MEM_W), D_MODEL ** -0.5),
        'g_q_mem': gain(ks[15], (DEPTH, MEM_HD)),
        'g_k_mem': gain(ks[16], (DEPTH, MEM_HD)),
        'w_br_gla': nrm(ks[17], (DEPTH, GLA_VAL, D_MODEL), GLA_VAL ** -0.5),
        'w_br_gqa': nrm(ks[18], (DEPTH, GQA_Q, D_MODEL), GQA_Q ** -0.5),
        'w_br_mem': nrm(ks[19], (DEPTH, MEM_W, D_MODEL), MEM_W ** -0.5),
        'w_out': nrm(ks[20], (DEPTH, D_MODEL, D_MODEL), D_MODEL ** -0.5),
        'g_ffn': gain(ks[21], (DEPTH, D_MODEL)),
        'w_router_group': nrm(ks[22], (DEPTH, D_MODEL, N_GROUPS), D_MODEL ** -0.5),
        'b_router_group': nrm(ks[23], (DEPTH, N_GROUPS), 0.01),
        'w_router_expert': nrm(ks[24], (DEPTH, D_MODEL, N_EXPERTS), D_MODEL ** -0.5),
        'b_router_expert': nrm(ks[25], (DEPTH, N_EXPERTS), 0.01),
        'w_exp_gate': nrm(ks[26], (DEPTH, N_EXPERTS, D_MODEL, D_EXPERT), D_MODEL ** -0.5),
        'w_exp_up': nrm(ks[27], (DEPTH, N_EXPERTS, D_MODEL, D_EXPERT), D_MODEL ** -0.5),
        'w_exp_down': nrm(ks[28], (DEPTH, N_EXPERTS, D_EXPERT, D_MODEL), D_EXPERT ** -0.5),
    }


def reference(x_prompt, x_sample, mem_prompt, mem_sample, g_mix, w_in, gla_a2_fwd, gla_ab_fwd, gla_a2_bwd, gla_ab_bwd,
              g_gla_out, g_q_gqa, g_k_gqa, g_mem_norm, w_mem_kv, g_q_mem, g_k_mem, w_br_gla, w_br_gqa, w_br_mem,
              w_out, g_ffn, w_router_group, b_router_group, w_router_expert, b_router_expert,
              w_exp_gate, w_exp_up, w_exp_down):
    y_prompt = x_prompt
    y_sample = x_sample
    for l in range(DEPTH):
        layer_params = (g_mix[l], w_in[l], gla_a2_fwd[l], gla_ab_fwd[l], gla_a2_bwd[l], gla_ab_bwd[l], g_gla_out[l],
                        g_q_gqa[l], g_k_gqa[l], g_mem_norm[l], w_mem_kv[l], g_q_mem[l], g_k_mem[l],
                        w_br_gla[l], w_br_gqa[l], w_br_mem[l], w_out[l], g_ffn[l],
                        w_router_group[l], b_router_group[l], w_router_expert[l], b_router_expert[l],
                        w_exp_gate[l], w_exp_up[l], w_exp_down[l])
        y_prompt = hybrid_layer(y_prompt, mem_prompt, *layer_params)
        y_sample = hybrid_layer(y_sample, mem_sample, *layer_params)
    return (y_prompt, y_sample)
```

```python
import functools

import numpy as np
import jax
import jax.numpy as jnp
from jax import lax
from jax.experimental import pallas as pl
from jax.experimental.pallas import tpu as pltpu

F32 = jnp.float32
BF16 = jnp.bfloat16

GRID_W = 64
GLA_HEADS, GLA_DK, GLA_DV = 4, 128, 256
GLA_KEY, GLA_VAL = GLA_HEADS * GLA_DK, GLA_HEADS * GLA_DV
GLA_RANK, GLA_TAU, GLA_CHUNK = 16, 16.0, 64
GQA_HEADS, GQA_KV_HEADS, HEAD_DIM = 8, 2, 128
GQA_GROUP = GQA_HEADS // GQA_KV_HEADS
GQA_Q, GQA_KV = GQA_HEADS * HEAD_DIM, GQA_KV_HEADS * HEAD_DIM
ROPE_THETA = 10000.0
MEM_HEADS, MEM_HD = 4, 256
MEM_W = MEM_HEADS * MEM_HD
N_GROUPS, EXPERTS_PER_GROUP, TOP_K, D_EXPERT = 4, 8, 2, 512
N_EXPERTS = N_GROUPS * EXPERTS_PER_GROUP
EPS = 1e-6

LANES = 128
VMEM_LIMIT_BYTES = 56 * 1024 * 1024


def _proj_layout(d):
    off = {}
    c = 0
    for name, w in (("mg", 3 * d), ("gv", GLA_VAL), ("gg", GLA_VAL), ("aq", GQA_Q), ("mq", MEM_W),
                    ("gq", GLA_KEY), ("gk", GLA_KEY), ("ak", GQA_KV), ("av", GQA_KV)):
        off[name] = c
        c += w
    off["total"] = c
    return off


def _cparams(sem, vmem=VMEM_LIMIT_BYTES):
    return pltpu.CompilerParams(dimension_semantics=sem, vmem_limit_bytes=vmem)


def _rms(x, g):
    ms = jnp.mean(x * x, axis=-1, keepdims=True)
    return x * lax.rsqrt(ms + EPS) * g


def _dot(a, b):
    return jnp.dot(a, b, preferred_element_type=F32)


def _dot_nt(a, b):
    return lax.dot_general(a, b, (((1,), (1,)), ((), ())), preferred_element_type=F32)


def _dot_tn(a, b):
    return lax.dot_general(a, b, (((0,), (0,)), ((), ())), preferred_element_type=F32)


def _sigmoid(x):
    return 1.0 / (1.0 + jnp.exp(-x))


def _pack_pair(lo, hi):
    lo_b = pltpu.bitcast(lo.astype(BF16).astype(F32), jnp.uint32)
    hi_b = pltpu.bitcast(hi.astype(BF16).astype(F32), jnp.uint32)
    return (hi_b & jnp.uint32(0xFFFF0000)) | (lo_b >> 16)


def _unpack_pair(w):
    lo = pltpu.bitcast(w << 16, F32)
    hi = pltpu.bitcast(w & jnp.uint32(0xFFFF0000), F32)
    return lo, hi


def _inproj_kernel(xp_ref, xs_ref, g_ref, w_ref, wga_ref, o_ref, ga_ref, h_ref, *, npb):
    i = pl.program_id(0)
    j = pl.program_id(1)

    def norm(x_ref):
        hb = _rms(x_ref[...], g_ref[...]).astype(BF16)
        h_ref[...] = hb
        ga_ref[...] = _dot(hb, wga_ref[...])

    @pl.when(jnp.logical_and(j == 0, i < npb))
    def _():
        norm(xp_ref)

    @pl.when(jnp.logical_and(j == 0, i >= npb))
    def _():
        norm(xs_ref)

    o_ref[...] = _dot(h_ref[...], w_ref[...]).astype(BF16)


def _inproj(xp, xs, g, w_main, w_ga, tm=1024, tn=512):
    tp, d = xp.shape
    ts = xs.shape[0]
    t = tp + ts
    nc = w_main.shape[1]
    npb = tp // tm
    grid = (t // tm, nc // tn)
    return pl.pallas_call(
        functools.partial(_inproj_kernel, npb=npb),
        grid=grid,
        in_specs=[
            pl.BlockSpec((tm, d), lambda i, j: (jnp.minimum(i, npb - 1), 0)),
            pl.BlockSpec((tm, d), lambda i, j: (jnp.maximum(i - npb, 0), 0)),
            pl.BlockSpec((1, d), lambda i, j: (0, 0)),
            pl.BlockSpec((d, tn), lambda i, j: (0, j)),
            pl.BlockSpec((d, LANES), lambda i, j: (0, 0)),
        ],
        out_specs=[
            pl.BlockSpec((tm, tn), lambda i, j: (i, j)),
            pl.BlockSpec((tm, LANES), lambda i, j: (i, 0)),
        ],
        out_shape=[jax.ShapeDtypeStruct((t, nc), BF16), jax.ShapeDtypeStruct((t, LANES), F32)],
        scratch_shapes=[pltpu.VMEM((tm, d), BF16)],
        compiler_params=_cparams(("arbitrary", "arbitrary")),
        name="inproj",
    )(xp, xs, g, w_main, w_ga)


GLA_TB = 512


def _gla_kernel(rb_ref, first_ref, q_ref, k_ref, v_ref, ga_ref, a2_ref, ab_ref, o_ref, s_ref, *, reverse):
    j = pl.program_id(0)
    tb, c = GLA_TB, GLA_CHUNK
    nch = tb // c

    @pl.when(first_ref[j] == 1)
    def _():
        s_ref[...] = jnp.zeros_like(s_ref)

    r = lax.broadcasted_iota(jnp.int32, (tb, tb), 0)
    cc = lax.broadcasted_iota(jnp.int32, (tb, tb), 1)
    same = (r >> 6) == (cc >> 6)
    if reverse:
        tri = (cc >= r).astype(BF16)
        amask = jnp.logical_and(same, cc > r)
    else:
        tri = (cc <= r).astype(BF16)
        amask = jnp.logical_and(same, cc <= r)

    gab = ga_ref[...].astype(BF16)
    scale = GLA_DK ** -0.5
    for h in range(GLA_HEADS):
        z = _dot(gab, a2_ref[h]) + ab_ref[:, h * GLA_DK:(h + 1) * GLA_DK]
        la = (jnp.minimum(z, 0.0) - jnp.log1p(jnp.exp(-jnp.abs(z)))) * (1.0 / GLA_TAU)
        la_hi = la.astype(BF16)
        la_lo = (la - la_hi.astype(F32)).astype(BF16)
        pre = _dot(tri, la_hi) + _dot(tri, la_lo)
        bs, tots = [], []
        for ci in range(nch):
            r0 = ci * c
            if reverse:
                base = pre[r0 + c:r0 + c + 1] if ci < nch - 1 else jnp.zeros((1, GLA_DK), F32)
                tot = pre[r0:r0 + 1] - base
            else:
                base = pre[r0 - 1:r0] if ci > 0 else jnp.zeros((1, GLA_DK), F32)
                tot = pre[r0 + c - 1:r0 + c] - base
            bs.append(pre[r0:r0 + c] - base)
            tots.append(tot)
        b = jnp.concatenate(bs, axis=0)
        btot = jnp.concatenate([jnp.broadcast_to(tt, (c, GLA_DK)) for tt in tots], axis=0)

        q = q_ref[:, h * GLA_DK:(h + 1) * GLA_DK].astype(F32) * scale
        k = k_ref[:, h * GLA_DK:(h + 1) * GLA_DK].astype(F32)
        v = v_ref[:, h * GLA_DV:(h + 1) * GLA_DV]
        q_e = (q * jnp.exp(b)).astype(BF16)
        k_e = (k * jnp.exp(-b)).astype(BF16)
        k_d = (k * jnp.exp(btot - b)).astype(BF16)
        att = jnp.where(amask, _dot_nt(q_e, k_e), 0.0).astype(BF16)
        o_intra = _dot(att, v)

        st = s_ref[h]
        outs = [None] * nch
        order = range(nch - 1, -1, -1) if reverse else range(nch)
        for ci in order:
            r0 = ci * c
            o_inter = _dot_nt(q_e[r0:r0 + c], st.astype(BF16))
            outs[ci] = o_intra[r0:r0 + c] + o_inter
            kvt = _dot_tn(v[r0:r0 + c], k_d[r0:r0 + c])
            st = st * jnp.exp(tots[ci]) + kvt
        s_ref[h] = st
        o_ref[:, h * GLA_DV:(h + 1) * GLA_DV] = jnp.concatenate(outs, axis=0).astype(BF16)


def _gla(proj, ga, a2p, ab, lay, seqs, reverse):
    t = proj.shape[0]
    tb = GLA_TB
    rb, first = [], []
    for (start, length) in seqs:
        blocks = list(range(start // tb, (start + length) // tb))
        if reverse:
            blocks = blocks[::-1]
        rb += blocks
        first += [1] + [0] * (len(blocks) - 1)
    rb = jnp.asarray(np.array(rb, np.int32))
    first = jnp.asarray(np.array(first, np.int32))
    qb, kb, vb = lay["gq"] // GLA_KEY, lay["gk"] // GLA_KEY, lay["gv"] // GLA_VAL
    grid_spec = pltpu.PrefetchScalarGridSpec(
        num_scalar_prefetch=2,
        grid=(t // tb,),
        in_specs=[
            pl.BlockSpec((tb, GLA_KEY), lambda j, rbr, fr: (rbr[j], qb)),
            pl.BlockSpec((tb, GLA_KEY), lambda j, rbr, fr: (rbr[j], kb)),
            pl.BlockSpec((tb, GLA_VAL), lambda j, rbr, fr: (rbr[j], vb)),
            pl.BlockSpec((tb, LANES), lambda j, rbr, fr: (rbr[j], 0)),
            pl.BlockSpec((GLA_HEADS, LANES, GLA_DK), lambda j, rbr, fr: (0, 0, 0)),
            pl.BlockSpec((1, GLA_KEY), lambda j, rbr, fr: (0, 0)),
        ],
        out_specs=pl.BlockSpec((tb, GLA_VAL), lambda j, rbr, fr: (rbr[j], 0)),
        scratch_shapes=[pltpu.VMEM((GLA_HEADS, GLA_DV, GLA_DK), F32)],
    )
    return pl.pallas_call(
        functools.partial(_gla_kernel, reverse=reverse),
        grid_spec=grid_spec,
        out_shape=jax.ShapeDtypeStruct((t, GLA_VAL), BF16),
        compiler_params=_cparams(("arbitrary",)),
        name="gla_bwd" if reverse else "gla_fwd",
    )(rb, first, proj, proj, proj, ga, a2p, ab)


def _qkprep_kernel(pb_ref, aq_ref, ak_ref, cos_ref, sin_ref, gq_ref, gk_ref, qa_ref, ka_ref):
    cosv = cos_ref[...]
    sinv = sin_ref[...]
    scale = HEAD_DIM ** -0.5

    def one(x, g):
        xn = _rms(x.astype(F32), g)
        return xn * cosv + pltpu.roll(xn, HEAD_DIM // 2, 1) * sinv

    for h in range(GQA_HEADS):
        sl = slice(h * HEAD_DIM, (h + 1) * HEAD_DIM)
        qa_ref[:, sl] = (one(aq_ref[:, sl], gq_ref[...]) * scale).astype(BF16)
    for h in range(GQA_KV_HEADS):
        sl = slice(h * HEAD_DIM, (h + 1) * HEAD_DIM)
        ka_ref[:, sl] = one(ak_ref[:, sl], gk_ref[...]).astype(BF16)


def _qkprep(proj, cos_t, sin_t, gq, gk, lay, seqs, tm=512):
    t = proj.shape[0]
    pb = []
    for (start, length) in seqs:
        pb += list(range(length // tm))
    pb = jnp.asarray(np.array(pb, np.int32))
    aqb, akb = lay["aq"] // GQA_Q, lay["ak"] // GQA_KV
    grid_spec = pltpu.PrefetchScalarGridSpec(
        num_scalar_prefetch=1,
        grid=(t // tm,),
        in_specs=[
            pl.BlockSpec((tm, GQA_Q), lambda i, p: (i, aqb)),
            pl.BlockSpec((tm, GQA_KV), lambda i, p: (i, akb)),
            pl.BlockSpec((tm, HEAD_DIM), lambda i, p: (p[i], 0)),
            pl.BlockSpec((tm, HEAD_DIM), lambda i, p: (p[i], 0)),
            pl.BlockSpec((1, HEAD_DIM), lambda i, p: (0, 0)),
            pl.BlockSpec((1, HEAD_DIM), lambda i, p: (0, 0)),
        ],
        out_specs=[
            pl.BlockSpec((tm, GQA_Q), lambda i, p: (i, 0)),
            pl.BlockSpec((tm, GQA_KV), lambda i, p: (i, 0)),
        ],
    )
    return pl.pallas_call(
        _qkprep_kernel,
        grid_spec=grid_spec,
        out_shape=[jax.ShapeDtypeStruct((t, GQA_Q), BF16), jax.ShapeDtypeStruct((t, GQA_KV), BF16)],
        compiler_params=_cparams(("arbitrary",)),
        name="qkprep",
    )(pb, proj, proj, cos_t, sin_t, gq, gk)


def _flash_kernel(q_ref, k_ref, v_ref, prev_ref, o_ref, m_ref, l_ref, acc_ref, *, tq, tk, nk):
    del prev_ref
    q = jnp.concatenate([q_ref[:, g * HEAD_DIM:(g + 1) * HEAD_DIM] for g in range(GQA_GROUP)], axis=0)
    m_ref[...] = jnp.full(m_ref.shape, -jnp.inf, F32)
    l_ref[...] = jnp.zeros(l_ref.shape, F32)
    acc_ref[...] = jnp.zeros(acc_ref.shape, F32)

    def step(ci, carry):
        off = pl.multiple_of(ci * tk, tk)
        kc = k_ref[pl.ds(off, tk), :]
        vc = v_ref[pl.ds(off, tk), :]
        s = _dot_nt(q, kc)
        m_prev = m_ref[...]
        m_new = jnp.maximum(m_prev, jnp.max(s, axis=1, keepdims=True))
        alpha = jnp.exp(m_prev - m_new)
        p = jnp.exp(s - m_new)
        l_ref[...] = alpha * l_ref[...] + jnp.sum(p, axis=1, keepdims=True)
        acc_ref[...] = alpha * acc_ref[...] + _dot(p.astype(BF16), vc)
        m_ref[...] = m_new
        return carry

    lax.fori_loop(0, nk, step, 0)
    out = acc_ref[...] / l_ref[...]
    for g in range(GQA_GROUP):
        o_ref[:, g * HEAD_DIM:(g + 1) * HEAD_DIM] = out[g * tq:(g + 1) * tq].astype(BF16)


def _flash_group(qa, ka, proj, lay, start, nseq, length, prev, tq=256, tk=512):
    t = qa.shape[0]
    tk = min(tk, length)
    nqt = length // tq
    qb0 = start // tq
    sb0 = start // length
    avb = lay["av"] // HEAD_DIM
    gw = GQA_GROUP * HEAD_DIM
    in_specs = [
        pl.BlockSpec((tq, gw), lambda s, i, kv: (qb0 + s * nqt + i, kv)),
        pl.BlockSpec((length, HEAD_DIM), lambda s, i, kv: (sb0 + s, kv)),
        pl.BlockSpec((length, HEAD_DIM), lambda s, i, kv: (sb0 + s, avb + kv)),
        pl.BlockSpec(memory_space=pl.ANY),
    ]
    args = [qa, ka, proj, prev]
    aliases = {3: 0}
    return pl.pallas_call(
        functools.partial(_flash_kernel, tq=tq, tk=tk, nk=length // tk),
        grid=(nseq, nqt, GQA_KV_HEADS),
        in_specs=in_specs,
        out_specs=pl.BlockSpec((tq, gw), lambda s, i, kv: (qb0 + s * nqt + i, kv)),
        out_shape=jax.ShapeDtypeStruct((t, GQA_Q), BF16),
        scratch_shapes=[pltpu.VMEM((GQA_GROUP * tq, 1), F32), pltpu.VMEM((GQA_GROUP * tq, 1), F32),
                        pltpu.VMEM((GQA_GROUP * tq, HEAD_DIM), F32)],
        input_output_aliases=aliases,
        compiler_params=_cparams(("arbitrary", "arbitrary", "arbitrary")),
        name="gqa_flash",
    )(*args)


def _memkv_kernel(mem_ref, g_ref, w_ref, gk_ref, km_ref, vm_ref):
    hb = _rms(mem_ref[...], g_ref[...]).astype(BF16)
    kv = _dot(hb, w_ref[...])
    for h in range(MEM_HEADS):
        sl = slice(h * MEM_HD, (h + 1) * MEM_HD)
        km_ref[:, sl] = _rms(kv[:, sl], gk_ref[...]).astype(BF16)
    vm_ref[...] = kv[:, MEM_W:].astype(BF16)


def _memkv(mem, g, w, gk):
    rows, d = mem.shape
    m = 256
    return pl.pallas_call(
        _memkv_kernel,
        grid=(rows // m,),
        in_specs=[
            pl.BlockSpec((m, d), lambda i: (i, 0)),
            pl.BlockSpec((1, d), lambda i: (0, 0)),
            pl.BlockSpec((d, 2 * MEM_W), lambda i: (0, 0)),
            pl.BlockSpec((1, MEM_HD), lambda i: (0, 0)),
        ],
        out_specs=[pl.BlockSpec((m, MEM_W), lambda i: (i, 0)), pl.BlockSpec((m, MEM_W), lambda i: (i, 0))],
        out_shape=[jax.ShapeDtypeStruct((rows, MEM_W), BF16), jax.ShapeDtypeStruct((rows, MEM_W), BF16)],
        compiler_params=_cparams(("arbitrary",)),
        name="memkv",
    )(mem, g, w, gk)


def _memattn_kernel(sq_ref, mq_ref, km_ref, vm_ref, gq_ref, o_ref):
    scale = MEM_HD ** -0.5
    for h in range(MEM_HEADS):
        sl = slice(h * MEM_HD, (h + 1) * MEM_HD)
        qn = (_rms(mq_ref[:, sl].astype(F32), gq_ref[...]) * scale).astype(BF16)
        s = _dot_nt(qn, km_ref[:, sl])
        m = jnp.max(s, axis=1, keepdims=True)
        p = jnp.exp(s - m)
        l = jnp.sum(p, axis=1, keepdims=True)
        o_ref[:, sl] = (_dot(p.astype(BF16), vm_ref[:, sl]) / l).astype(BF16)


def _memattn(proj, km, vm, gq, lay, seqs, n_mem, tm=512):
    t = proj.shape[0]
    sq = []
    for si, (start, length) in enumerate(seqs):
        sq += [si] * (length // tm)
    sq = jnp.asarray(np.array(sq, np.int32))
    mqb = lay["mq"] // MEM_W
    grid_spec = pltpu.PrefetchScalarGridSpec(
        num_scalar_prefetch=1,
        grid=(t // tm,),
        in_specs=[
            pl.BlockSpec((tm, MEM_W), lambda i, s: (i, mqb)),
            pl.BlockSpec((n_mem, MEM_W), lambda i, s: (s[i], 0)),
            pl.BlockSpec((n_mem, MEM_W), lambda i, s: (s[i], 0)),
            pl.BlockSpec((1, MEM_HD), lambda i, s: (0, 0)),
        ],
        out_specs=pl.BlockSpec((tm, MEM_W), lambda i, s: (i, 0)),
    )
    return pl.pallas_call(
        _memattn_kernel,
        grid_spec=grid_spec,
        out_shape=jax.ShapeDtypeStruct((t, MEM_W), BF16),
        compiler_params=_cparams(("arbitrary",)),
        name="memattn",
    )(sq, proj, km, vm, gq)


def _merge_kernel(of_ref, ob_ref, gg_ref, og_ref, om_ref, mg_ref, ggla_ref, w0_ref, w1_ref, w2_ref, o_ref, *, d):
    o = of_ref[...].astype(F32) + ob_ref[...].astype(F32)
    gg = gg_ref[...].astype(F32)
    parts = []
    for h in range(GLA_HEADS):
        sl = slice(h * GLA_DV, (h + 1) * GLA_DV)
        gh = gg[:, sl]
        parts.append((_rms(o[:, sl], ggla_ref[...]) * (gh * _sigmoid(gh))).astype(BF16))
    y0 = jnp.concatenate(parts, axis=1)
    acc = _sigmoid(mg_ref[:, 0:d].astype(F32)) * _dot(y0, w0_ref[...])
    acc = acc + _sigmoid(mg_ref[:, d:2 * d].astype(F32)) * _dot(og_ref[...], w1_ref[...])
    acc = acc + _sigmoid(mg_ref[:, 2 * d:3 * d].astype(F32)) * _dot(om_ref[...], w2_ref[...])
    o_ref[...] = acc.astype(BF16)


def _const_spec(shape):
    nd = len(shape)
    return pl.BlockSpec(shape, lambda i: (0,) * nd, pipeline_mode=pl.Buffered(1))


def _merge(o_f, o_b, proj, o_gqa, o_mem, g_gla, w0, w1, w2, lay, d, tm=512):
    t = proj.shape[0]
    ggb = lay["gg"] // GLA_VAL
    return pl.pallas_call(
        functools.partial(_merge_kernel, d=d),
        grid=(t // tm,),
        in_specs=[
            pl.BlockSpec((tm, GLA_VAL), lambda i: (i, 0)),
            pl.BlockSpec((tm, GLA_VAL), lambda i: (i, 0)),
            pl.BlockSpec((tm, GLA_VAL), lambda i: (i, ggb)),
            pl.BlockSpec((tm, GQA_Q), lambda i: (i, 0)),
            pl.BlockSpec((tm, MEM_W), lambda i: (i, 0)),
            pl.BlockSpec((tm, 3 * d), lambda i: (i, 0)),
            _const_spec((1, GLA_DV)),
            _const_spec((GLA_VAL, d)),
            _const_spec((GQA_Q, d)),
            _const_spec((MEM_W, d)),
        ],
        out_specs=pl.BlockSpec((tm, d), lambda i: (i, 0)),
        out_shape=jax.ShapeDtypeStruct((t, d), BF16),
        compiler_params=_cparams(("arbitrary",)),
        name="merge",
    )(o_f, o_b, proj, o_gqa, o_mem, proj, g_gla, w0, w1, w2)


ROUTER_ROWS = 40


def _outproj_kernel(mrg_ref, xp_ref, xs_ref, wo_ref, gf_ref, wrh_ref, wrl_ref, br_ref,
                    x1_ref, h2p_ref, lg_ref, *, npb, d):
    i = pl.program_id(0)
    x = jnp.where(i < npb, xp_ref[...], xs_ref[...])
    x1 = x + _dot(mrg_ref[...], wo_ref[...])
    x1_ref[...] = x1
    h2 = _rms(x1, gf_ref[...])
    h_hi = h2.astype(BF16)
    h_lo = (h2 - h_hi.astype(F32)).astype(BF16)
    lg = _dot_nt(wrh_ref[...], h_hi) + _dot_nt(wrh_ref[...], h_lo) + _dot_nt(wrl_ref[...], h_hi)
    lg_ref[...] = lg + br_ref[:, 0:1]
    half = d // 2
    h2p_ref[...] = _pack_pair(h2[:, :half], h2[:, half:])


def _outproj(merged, xp, xs, wo, gf, wr_hi, wr_lo, br, tm=512):
    tp, d = xp.shape
    t = merged.shape[0]
    npb = tp // tm
    return pl.pallas_call(
        functools.partial(_outproj_kernel, npb=npb, d=d),
        grid=(t // tm,),
        in_specs=[
            pl.BlockSpec((tm, d), lambda i: (i, 0)),
            pl.BlockSpec((tm, d), lambda i: (jnp.minimum(i, npb - 1), 0)),
            pl.BlockSpec((tm, d), lambda i: (jnp.maximum(i - npb, 0), 0)),
            _const_spec((d, d)),
            _const_spec((1, d)),
            _const_spec((ROUTER_ROWS, d)),
            _const_spec((ROUTER_ROWS, d)),
            _const_spec((ROUTER_ROWS, LANES)),
        ],
        out_specs=[
            pl.BlockSpec((tm, d), lambda i: (i, 0)),
            pl.BlockSpec((tm, d // 2), lambda i: (i, 0)),
            pl.BlockSpec((ROUTER_ROWS, tm), lambda i: (0, i)),
        ],
        out_shape=[jax.ShapeDtypeStruct((t, d), F32), jax.ShapeDtypeStruct((t, d // 2), jnp.uint32),
                   jax.ShapeDtypeStruct((ROUTER_ROWS, t), F32)],
        compiler_params=_cparams(("arbitrary",)),
        name="outproj",
    )(merged, xp, xs, wo, gf, wr_hi, wr_lo, br)


def _route_kernel(lg_ref, info_ref, w_ref, cnt_ref, carry_ref, *, tm):
    i = pl.program_id(0)

    @pl.when(i == 0)
    def _():
        carry_ref[...] = jnp.zeros_like(carry_ref)

    neg = -jnp.inf
    row8 = lax.broadcasted_iota(jnp.int32, (8, tm), 0)
    g = jnp.where(row8 < N_GROUPS, lg_ref[0:8, :], neg)
    gmax = jnp.max(g, axis=0, keepdims=True)
    gsum = jnp.sum(jnp.exp(g - gmax), axis=0, keepdims=True)
    p_grp = 1.0 / gsum
    grp = jnp.min(jnp.where(g == gmax, row8, 8), axis=0, keepdims=True)

    e8 = lg_ref[8:16, :]
    for gi in range(1, N_GROUPS):
        e8 = jnp.where(grp == gi, lg_ref[8 + 8 * gi:16 + 8 * gi, :], e8)
    m1 = jnp.max(e8, axis=0, keepdims=True)
    i1 = jnp.min(jnp.where(e8 == m1, row8, 8), axis=0, keepdims=True)
    e8b = jnp.where(row8 == i1, neg, e8)
    m2 = jnp.max(e8b, axis=0, keepdims=True)
    i2 = jnp.min(jnp.where(e8b == m2, row8, 8), axis=0, keepdims=True)
    zsum = jnp.sum(jnp.exp(e8 - m1), axis=0, keepdims=True)
    p1 = 1.0 / zsum
    p2 = jnp.exp(m2 - m1) / zsum
    den = p1 + p2
    w1 = p_grp * p1 / den
    w2 = p_grp * p2 / den
    eid1 = grp * EXPERTS_PER_GROUP + i1
    eid2 = grp * EXPERTS_PER_GROUP + i2

    rowe = lax.broadcasted_iota(jnp.int32, (N_EXPERTS, tm), 0)
    oh1 = (rowe == eid1)
    oh2 = (rowe == eid2)
    tr = lax.broadcasted_iota(jnp.int32, (tm, tm), 0)
    tc = lax.broadcasted_iota(jnp.int32, (tm, tm), 1)
    upper = (tr < tc).astype(BF16)
    cum1 = _dot(oh1.astype(BF16), upper)
    cum2 = _dot(oh2.astype(BF16), upper)
    oh1f = oh1.astype(F32)
    oh2f = oh2.astype(F32)
    tot1 = jnp.sum(oh1f, axis=1, keepdims=True)
    tot2 = jnp.sum(oh2f, axis=1, keepdims=True)
    carry = carry_ref[:, 0:1]
    rank1 = jnp.sum(oh1f * (cum1 + carry), axis=0, keepdims=True)
    rank2 = jnp.sum(oh2f * (cum2 + carry + tot1), axis=0, keepdims=True)
    new_carry = carry + tot1 + tot2
    carry_ref[...] = jnp.broadcast_to(new_carry, carry_ref.shape)
    cnt_ref[...] = jnp.broadcast_to(new_carry, cnt_ref.shape)

    zi = jnp.zeros((1, tm), jnp.int32)
    info_ref[...] = jnp.concatenate(
        [eid1, eid2, rank1.astype(jnp.int32), rank2.astype(jnp.int32), zi, zi, zi, zi], axis=0)
    zf = jnp.zeros((1, tm), F32)
    w_ref[...] = jnp.concatenate([w1, w2, zf, zf, zf, zf, zf, zf], axis=0)


def _route(logits_t, tm=512):
    t = logits_t.shape[1]
    return pl.pallas_call(
        functools.partial(_route_kernel, tm=tm),
        grid=(t // tm,),
        in_specs=[pl.BlockSpec((ROUTER_ROWS, tm), lambda i: (0, i))],
        out_specs=[
            pl.BlockSpec((8, tm), lambda i: (0, i)),
            pl.BlockSpec((8, tm), lambda i: (0, i)),
            pl.BlockSpec((N_EXPERTS, LANES), lambda i: (0, 0)),
        ],
        out_shape=[jax.ShapeDtypeStruct((8, t), jnp.int32), jax.ShapeDtypeStruct((8, t), F32),
                   jax.ShapeDtypeStruct((N_EXPERTS, LANES), F32)],
        scratch_shapes=[pltpu.VMEM((N_EXPERTS, LANES), F32)],
        compiler_params=_cparams(("arbitrary",)),
        name="route",
    )(logits_t)


def _dispatch_kernel(dest_ref, h_ref, xs_in_ref, xs_ref, sem, *, tm):
    del xs_in_ref

    def row_copy(t, d):
        return pltpu.make_async_copy(h_ref.at[pl.ds(t, 1)], xs_ref.at[pl.ds(d, 1)], sem)

    def issue(t, carry):
        row_copy(t, dest_ref[0, t]).start()
        row_copy(t, dest_ref[1, t]).start()
        return carry

    lax.fori_loop(0, tm, issue, 0)

    def drain(t, carry):
        row_copy(0, 0).wait()
        row_copy(0, 0).wait()
        return carry

    lax.fori_loop(0, tm, drain, 0)


def _dispatch(dest, h2p, xs_init, tm=512):
    t, hw = h2p.shape
    return pl.pallas_call(
        functools.partial(_dispatch_kernel, tm=tm),
        grid=(t // tm,),
        in_specs=[
            pl.BlockSpec((8, tm), lambda i: (0, i), memory_space=pltpu.SMEM),
            pl.BlockSpec((tm, hw), lambda i: (i, 0)),
            pl.BlockSpec(memory_space=pl.ANY),
        ],
        out_specs=pl.BlockSpec(memory_space=pl.ANY),
        out_shape=jax.ShapeDtypeStruct(xs_init.shape, xs_init.dtype),
        scratch_shapes=[pltpu.SemaphoreType.DMA(())],
        input_output_aliases={2: 0},
        compiler_params=_cparams(("arbitrary",)),
        name="dispatch",
    )(dest, h2p, xs_init)


MOE_BLK = 256


def _expert_kernel(be_ref, nu_ref, x_ref, wg_ref, wu_ref, wd_ref, y_ref, *, d):
    b = pl.program_id(0)

    @pl.when(b < nu_ref[0])
    def _():
        lo, hi = _unpack_pair(x_ref[...])
        x = jnp.concatenate([lo, hi], axis=1).astype(BF16)
        gate = _dot(x, wg_ref[0])
        up = _dot(x, wu_ref[0])
        hid = (gate * _sigmoid(gate) * up).astype(BF16)
        y = _dot(hid, wd_ref[0])
        half = d // 2
        y_ref[...] = _pack_pair(y[:, :half], y[:, half:])

    @pl.when(b >= nu_ref[0])
    def _():
        y_ref[...] = jnp.zeros_like(y_ref)


def _experts(blk_expert, n_used, xs, wg, wu, wd, d):
    p, hw = xs.shape
    nb = p // MOE_BLK

    def xmap(b, be, nu):
        return (jnp.minimum(b, nu[0] - 1), 0)

    grid_spec = pltpu.PrefetchScalarGridSpec(
        num_scalar_prefetch=2,
        grid=(nb,),
        in_specs=[
            pl.BlockSpec((MOE_BLK, hw), xmap),
            pl.BlockSpec((1, d, D_EXPERT), lambda b, be, nu: (be[b], 0, 0)),
            pl.BlockSpec((1, d, D_EXPERT), lambda b, be, nu: (be[b], 0, 0)),
            pl.BlockSpec((1, D_EXPERT, d), lambda b, be, nu: (be[b], 0, 0)),
        ],
        out_specs=pl.BlockSpec((MOE_BLK, hw), lambda b, be, nu: (b, 0)),
    )
    return pl.pallas_call(
        functools.partial(_expert_kernel, d=d),
        grid_spec=grid_spec,
        out_shape=jax.ShapeDtypeStruct((p, hw), jnp.uint32),
        compiler_params=_cparams(("arbitrary",)),
        name="experts",
    )(blk_expert, n_used, xs, wg, wu, wd)


def _combine_kernel(dest_ref, x1_ref, wt_ref, yb_ref, op_ref, os_ref, buf_ref, sem, *, tm, npb):
    i = pl.program_id(0)

    def row_copy(k, t, d):
        return pltpu.make_async_copy(yb_ref.at[pl.ds(d, 1)], buf_ref.at[k, pl.ds(t, 1)], sem)

    def issue(t, carry):
        row_copy(0, t, dest_ref[0, t]).start()
        row_copy(1, t, dest_ref[1, t]).start()
        return carry

    lax.fori_loop(0, tm, issue, 0)

    def drain(t, carry):
        row_copy(0, 0, 0).wait()
        row_copy(0, 0, 0).wait()
        return carry

    lax.fori_loop(0, tm, drain, 0)

    w = wt_ref[...]
    a_lo, a_hi = _unpack_pair(buf_ref[0])
    b_lo, b_hi = _unpack_pair(buf_ref[1])
    w1 = w[:, 0:1]
    w2 = w[:, 1:2]
    moe = jnp.concatenate([a_lo * w1 + b_lo * w2, a_hi * w1 + b_hi * w2], axis=1)
    out = x1_ref[...] + moe

    @pl.when(i < npb)
    def _():
        op_ref[...] = out

    @pl.when(i >= npb)
    def _():
        os_ref[...] = out


def _combine(dest, x1, wt, yb, tp, tm=256):
    t, d = x1.shape
    hw = yb.shape[1]
    npb = tp // tm
    return pl.pallas_call(
        functools.partial(_combine_kernel, tm=tm, npb=npb),
        grid=(t // tm,),
        in_specs=[
            pl.BlockSpec((8, tm), lambda i: (0, i), memory_space=pltpu.SMEM),
            pl.BlockSpec((tm, d), lambda i: (i, 0)),
            pl.BlockSpec((tm, 8), lambda i: (i, 0)),
            pl.BlockSpec(memory_space=pl.ANY),
        ],
        out_specs=[
            pl.BlockSpec((tm, d), lambda i: (jnp.minimum(i, npb - 1), 0)),
            pl.BlockSpec((tm, d), lambda i: (jnp.maximum(i - npb, 0), 0)),
        ],
        out_shape=[jax.ShapeDtypeStruct((tp, d), F32), jax.ShapeDtypeStruct((t - tp, d), F32)],
        scratch_shapes=[pltpu.VMEM((2, tm, hw), jnp.uint32), pltpu.SemaphoreType.DMA(())],
        compiler_params=_cparams(("arbitrary",)),
        name="combine",
    )(dest, x1, wt, yb)


def _rope_tables(length):
    rows = length // GRID_W
    row = jnp.broadcast_to(jnp.arange(rows, dtype=F32)[:, None], (rows, GRID_W)).reshape(length)
    col = jnp.broadcast_to(jnp.arange(GRID_W, dtype=F32)[None, :], (rows, GRID_W)).reshape(length)
    axis_dim = HEAD_DIM // 2
    freqs = ROPE_THETA ** (-jnp.arange(0, axis_dim, 2, dtype=F32) / axis_dim)
    ang = jnp.concatenate([row[:, None] * freqs, col[:, None] * freqs], axis=-1)
    c, s = jnp.cos(ang), jnp.sin(ang)
    return jnp.concatenate([c, c], axis=1), jnp.concatenate([-s, s], axis=1)


def _layer(xp, xs, mem_all, seqs, n_mem, p):
    d = xp.shape[1]
    tp = xp.shape[0]
    lay = _proj_layout(d)

    w_in = p["w_in"]
    sizes = (GLA_KEY, GLA_KEY, GLA_VAL, GLA_VAL, GLA_RANK, GLA_RANK, GQA_Q, GQA_KV, GQA_KV, MEM_W, 3 * d)
    offs = np.concatenate([[0], np.cumsum(sizes)])
    seg = {n: w_in[:, offs[k]:offs[k + 1]] for k, n in enumerate(
        ("gq", "gk", "gv", "gg", "ga_f", "ga_b", "aq", "ak", "av", "mq", "mg"))}
    perm = np.concatenate([np.arange(0, HEAD_DIM, 2), np.arange(1, HEAD_DIM, 2)])
    perm_q = np.concatenate([h * HEAD_DIM + perm for h in range(GQA_HEADS)])
    perm_k = np.concatenate([h * HEAD_DIM + perm for h in range(GQA_KV_HEADS)])
    w_main = jnp.concatenate([seg["mg"], seg["gv"], seg["gg"], seg["aq"][:, perm_q], seg["mq"], seg["gq"],
                              seg["gk"], seg["ak"][:, perm_k], seg["av"]], axis=1).astype(BF16)
    w_ga = jnp.concatenate([seg["ga_f"], seg["ga_b"], jnp.zeros((d, LANES - 2 * GLA_RANK), F32)],
                           axis=1).astype(BF16)

    def a2_pad(a2, row0):
        z = jnp.zeros((LANES, GLA_KEY), F32).at[row0:row0 + GLA_RANK].set(a2)
        return z.reshape(LANES, GLA_HEADS, GLA_DK).transpose(1, 0, 2).astype(BF16)

    gq_perm = p["g_q_gqa"][perm][None, :]
    gk_perm = p["g_k_gqa"][perm][None, :]

    proj, ga = _inproj(xp, xs, p["g_mix"][None, :], w_main, w_ga)

    o_f = _gla(proj, ga, a2_pad(p["gla_a2_fwd"], 0), p["gla_ab_fwd"][None, :], lay, seqs, False)
    o_b = _gla(proj, ga, a2_pad(p["gla_a2_bwd"], GLA_RANK), p["gla_ab_bwd"][None, :], lay, seqs, True)

    max_len = max(length for _, length in seqs)
    cos_t, sin_t = _rope_tables(max_len)
    qa, ka = _qkprep(proj, cos_t, sin_t, gq_perm, gk_perm, lay, seqs)
    groups = []
    for (start, length) in seqs:
        if groups and groups[-1][2] == length and groups[-1][0] + groups[-1][1] * length == start:
            groups[-1][1] += 1
        else:
            groups.append([start, 1, length])
    o_gqa = jnp.zeros((proj.shape[0], GQA_Q), BF16)
    for (start, nseq, length) in groups:
        o_gqa = _flash_group(qa, ka, proj, lay, start, nseq, length, prev=o_gqa)

    km, vm = _memkv(mem_all, p["g_mem_norm"][None, :], p["w_mem_kv"].astype(BF16), p["g_k_mem"][None, :])
    o_mem = _memattn(proj, km, vm, p["g_q_mem"][None, :], lay, seqs, n_mem)

    merged = _merge(o_f, o_b, proj, o_gqa, o_mem, p["g_gla_out"][None, :], p["w_br_gla"].astype(BF16),
                    p["w_br_gqa"].astype(BF16), p["w_br_mem"].astype(BF16), lay, d)
    wr = jnp.zeros((ROUTER_ROWS, d), F32)
    wr = wr.at[0:N_GROUPS].set(p["w_router_group"].T).at[8:8 + N_EXPERTS].set(p["w_router_expert"].T)
    wr_hi = wr.astype(BF16)
    wr_lo = (wr - wr_hi.astype(F32)).astype(BF16)
    br = jnp.zeros((ROUTER_ROWS,), F32).at[0:N_GROUPS].set(p["b_router_group"]).at[8:8 + N_EXPERTS].set(
        p["b_router_expert"])
    br = jnp.broadcast_to(br[:, None], (ROUTER_ROWS, LANES))
    x1, h2p, logits_t = _outproj(merged, xp, xs, p["w_out"].astype(BF16), p["g_ffn"][None, :], wr_hi, wr_lo, br)

    info, wts, counts = _route(logits_t)
    t = x1.shape[0]
    cnt = counts[:, 0].astype(jnp.int32)
    padded = ((cnt + MOE_BLK - 1) // MOE_BLK) * MOE_BLK
    pad_ends = jnp.cumsum(padded)
    pad_starts = pad_ends - padded
    nb = (t * TOP_K) // MOE_BLK + N_EXPERTS
    n_used = (pad_ends[-1] // MOE_BLK).astype(jnp.int32)
    blk = jnp.arange(nb, dtype=jnp.int32)
    blk_expert = jnp.searchsorted(pad_ends, jnp.minimum(blk, n_used - 1) * MOE_BLK, side="right")
    blk_expert = jnp.clip(blk_expert, 0, N_EXPERTS - 1).astype(jnp.int32)
    dest = pad_starts[info[0:2]] + info[2:4]
    dest = jnp.concatenate([dest, jnp.zeros((6, t), jnp.int32)], axis=0)

    xs_sorted = _dispatch(dest, h2p, jnp.zeros((nb * MOE_BLK, d // 2), jnp.uint32))
    yb = _experts(blk_expert, n_used[None], xs_sorted, p["w_exp_gate"].astype(BF16),
                  p["w_exp_up"].astype(BF16), p["w_exp_down"].astype(BF16), d)
    return _combine(dest, x1, wts.T, yb, tp)


def kernel(x_prompt, x_sample, mem_prompt, mem_sample, g_mix, w_in, gla_a2_fwd, gla_ab_fwd, gla_a2_bwd, gla_ab_bwd, g_gla_out, g_q_gqa, g_k_gqa, g_mem_norm, w_mem_kv, g_q_mem, g_k_mem, w_br_gla, w_br_gqa, w_br_mem, w_out, g_ffn, w_router_group, b_router_group, w_router_expert, b_router_expert, w_exp_gate, w_exp_up, w_exp_down):
    bp, lp, d = x_prompt.shape
    bs, ls, _ = x_sample.shape
    n_mem = mem_prompt.shape[1]
    depth = g_mix.shape[0]
    seqs = [(b * lp, lp) for b in range(bp)] + [(bp * lp + b * ls, ls) for b in range(bs)]
    xp = x_prompt.reshape(bp * lp, d)
    xs = x_sample.reshape(bs * ls, d)
    mem_all = jnp.concatenate([mem_prompt.reshape(bp * n_mem, d), mem_sample.reshape(bs * n_mem, d)], axis=0)
    names = ("g_mix", "w_in", "gla_a2_fwd", "gla_ab_fwd", "gla_a2_bwd", "gla_ab_bwd", "g_gla_out", "g_q_gqa",
             "g_k_gqa", "g_mem_norm", "w_mem_kv", "g_q_mem", "g_k_mem", "w_br_gla", "w_br_gqa", "w_br_mem",
             "w_out", "g_ffn", "w_router_group", "b_router_group", "w_router_expert", "b_router_expert",
             "w_exp_gate", "w_exp_up", "w_exp_down")
    vals = (g_mix, w_in, gla_a2_fwd, gla_ab_fwd, gla_a2_bwd, gla_ab_bwd, g_gla_out, g_q_gqa, g_k_gqa,
            g_mem_norm, w_mem_kv, g_q_mem, g_k_mem, w_br_gla, w_br_gqa, w_br_mem, w_out, g_ffn,
            w_router_group, b_router_group, w_router_expert, b_router_expert, w_exp_gate, w_exp_up, w_exp_down)
    for layer in range(depth):
        p = {n: v[layer] for n, v in zip(names, vals)}
        xp, xs = _layer(xp, xs, mem_all, seqs, n_mem, p)
    return xp.reshape(bp, lp, d), xs.reshape(bs, ls, d)
```

```python
import functools

import numpy as np
import jax
import jax.numpy as jnp
from jax import lax
from jax.experimental import pallas as pl
from jax.experimental.pallas import tpu as pltpu

F32 = jnp.float32
BF16 = jnp.bfloat16

GRID_W = 64
GLA_HEADS, GLA_DK, GLA_DV = 4, 128, 256
GLA_KEY, GLA_VAL = GLA_HEADS * GLA_DK, GLA_HEADS * GLA_DV
GLA_RANK, GLA_TAU, GLA_CHUNK = 16, 16.0, 64
GQA_HEADS, GQA_KV_HEADS, HEAD_DIM = 8, 2, 128
GQA_GROUP = GQA_HEADS // GQA_KV_HEADS
GQA_Q, GQA_KV = GQA_HEADS * HEAD_DIM, GQA_KV_HEADS * HEAD_DIM
ROPE_THETA = 10000.0
MEM_HEADS, MEM_HD = 4, 256
MEM_W = MEM_HEADS * MEM_HD
N_GROUPS, EXPERTS_PER_GROUP, TOP_K, D_EXPERT = 4, 8, 2, 512
N_EXPERTS = N_GROUPS * EXPERTS_PER_GROUP
EPS = 1e-6

LANES = 128
VMEM_LIMIT_BYTES = 56 * 1024 * 1024


def _proj_layout(d):
    off = {}
    c = 0
    for name, w in (("mg", 3 * d), ("gv", GLA_VAL), ("gg", GLA_VAL), ("aq", GQA_Q), ("mq", MEM_W),
                    ("gq", GLA_KEY), ("gk", GLA_KEY), ("ak", GQA_KV), ("av", GQA_KV)):
        off[name] = c
        c += w
    off["total"] = c
    return off


def _cparams(sem, vmem=VMEM_LIMIT_BYTES):
    return pltpu.CompilerParams(dimension_semantics=sem, vmem_limit_bytes=vmem)


def _rms(x, g):
    ms = jnp.mean(x * x, axis=-1, keepdims=True)
    return x * lax.rsqrt(ms + EPS) * g


def _dot(a, b):
    return jnp.dot(a, b, preferred_element_type=F32)


def _dot_nt(a, b):
    return lax.dot_general(a, b, (((1,), (1,)), ((), ())), preferred_element_type=F32)


def _dot_tn(a, b):
    return lax.dot_general(a, b, (((0,), (0,)), ((), ())), preferred_element_type=F32)


def _sigmoid(x):
    return 1.0 / (1.0 + jnp.exp(-x))


def _pack_pair(lo, hi):
    lo_b = pltpu.bitcast(lo.astype(BF16).astype(F32), jnp.uint32)
    hi_b = pltpu.bitcast(hi.astype(BF16).astype(F32), jnp.uint32)
    return (hi_b & jnp.uint32(0xFFFF0000)) | (lo_b >> 16)


def _unpack_pair(w):
    lo = pltpu.bitcast(w << 16, F32)
    hi = pltpu.bitcast(w & jnp.uint32(0xFFFF0000), F32)
    return lo, hi


def _inproj_kernel(xp_ref, xs_ref, g_ref, w_ref, wga_ref, o_ref, ga_ref, h_ref, *, npb):
    i = pl.program_id(0)
    j = pl.program_id(1)

    def norm(x_ref):
        hb = _rms(x_ref[...], g_ref[...]).astype(BF16)
        h_ref[...] = hb
        ga_ref[...] = _dot(hb, wga_ref[...])

    @pl.when(jnp.logical_and(j == 0, i < npb))
    def _():
        norm(xp_ref)

    @pl.when(jnp.logical_and(j == 0, i >= npb))
    def _():
        norm(xs_ref)

    o_ref[...] = _dot(h_ref[...], w_ref[...]).astype(BF16)


def _inproj(xp, xs, g, w_main, w_ga, tm=1024, tn=512):
    tp, d = xp.shape
    ts = xs.shape[0]
    t = tp + ts
    nc = w_main.shape[1]
    npb = tp // tm
    grid = (t // tm, nc // tn)
    return pl.pallas_call(
        functools.partial(_inproj_kernel, npb=npb),
        grid=grid,
        in_specs=[
            pl.BlockSpec((tm, d), lambda i, j: (jnp.minimum(i, npb - 1), 0)),
            pl.BlockSpec((tm, d), lambda i, j: (jnp.maximum(i - npb, 0), 0)),
            pl.BlockSpec((1, d), lambda i, j: (0, 0)),
            pl.BlockSpec((d, tn), lambda i, j: (0, j)),
            pl.BlockSpec((d, LANES), lambda i, j: (0, 0)),
        ],
        out_specs=[
            pl.BlockSpec((tm, tn), lambda i, j: (i, j)),
            pl.BlockSpec((tm, LANES), lambda i, j: (i, 0)),
        ],
        out_shape=[jax.ShapeDtypeStruct((t, nc), BF16), jax.ShapeDtypeStruct((t, LANES), F32)],
        scratch_shapes=[pltpu.VMEM((tm, d), BF16)],
        compiler_params=_cparams(("arbitrary", "arbitrary")),
        name="inproj",
    )(xp, xs, g, w_main, w_ga)


GLA_TB = 512


def _gla_kernel(rb_ref, first_ref, q_ref, k_ref, v_ref, ga_ref, a2_ref, ab_ref, o_ref, s_ref, *, reverse):
    j = pl.program_id(0)
    tb, c = GLA_TB, GLA_CHUNK
    nch = tb // c

    @pl.when(first_ref[j] == 1)
    def _():
        s_ref[...] = jnp.zeros_like(s_ref)

    r = lax.broadcasted_iota(jnp.int32, (tb, tb), 0)
    cc = lax.broadcasted_iota(jnp.int32, (tb, tb), 1)
    same = (r >> 6) == (cc >> 6)
    if reverse:
        tri = (cc >= r).astype(BF16)
        amask = jnp.logical_and(same, cc > r)
    else:
        tri = (cc <= r).astype(BF16)
        amask = jnp.logical_and(same, cc <= r)

    gab = ga_ref[...].astype(BF16)
    scale = GLA_DK ** -0.5
    for h in range(GLA_HEADS):
        z = _dot(gab, a2_ref[h]) + ab_ref[:, h * GLA_DK:(h + 1) * GLA_DK]
        la = (jnp.minimum(z, 0.0) - jnp.log1p(jnp.exp(-jnp.abs(z)))) * (1.0 / GLA_TAU)
        la_hi = la.astype(BF16)
        la_lo = (la - la_hi.astype(F32)).astype(BF16)
        pre = _dot(tri, la_hi) + _dot(tri, la_lo)
        bs, tots = [], []
        for ci in range(nch):
            r0 = ci * c
            if reverse:
                base = pre[r0 + c:r0 + c + 1] if ci < nch - 1 else jnp.zeros((1, GLA_DK), F32)
                tot = pre[r0:r0 + 1] - base
            else:
                base = pre[r0 - 1:r0] if ci > 0 else jnp.zeros((1, GLA_DK), F32)
                tot = pre[r0 + c - 1:r0 + c] - base
            bs.append(pre[r0:r0 + c] - base)
            tots.append(tot)
        b = jnp.concatenate(bs, axis=0)
        btot = jnp.concatenate([jnp.broadcast_to(tt, (c, GLA_DK)) for tt in tots], axis=0)

        q = q_ref[:, h * GLA_DK:(h + 1) * GLA_DK].astype(F32) * scale
        k = k_ref[:, h * GLA_DK:(h + 1) * GLA_DK].astype(F32)
        v = v_ref[:, h * GLA_DV:(h + 1) * GLA_DV]
        q_e = (q * jnp.exp(b)).astype(BF16)
        k_e = (k * jnp.exp(-b)).astype(BF16)
        k_d = (k * jnp.exp(btot - b)).astype(BF16)
        att = jnp.where(amask, _dot_nt(q_e, k_e), 0.0).astype(BF16)
        o_intra = _dot(att, v)

        st = s_ref[h]
        outs = [None] * nch
        order = range(nch - 1, -1, -1) if reverse else range(nch)
        for ci in order:
            r0 = ci * c
            o_inter = _dot_nt(q_e[r0:r0 + c], st.astype(BF16))
            outs[ci] = o_intra[r0:r0 + c] + o_inter
            kvt = _dot_tn(v[r0:r0 + c], k_d[r0:r0 + c])
            st = st * jnp.exp(tots[ci]) + kvt
        s_ref[h] = st
        o_ref[:, h * GLA_DV:(h + 1) * GLA_DV] = jnp.concatenate(outs, axis=0).astype(BF16)


def _gla(proj, ga, a2p, ab, lay, seqs, reverse):
    t = proj.shape[0]
    tb = GLA_TB
    rb, first = [], []
    for (start, length) in seqs:
        blocks = list(range(start // tb, (start + length) // tb))
        if reverse:
            blocks = blocks[::-1]
        rb += blocks
        first += [1] + [0] * (len(blocks) - 1)
    rb = jnp.asarray(np.array(rb, np.int32))
    first = jnp.asarray(np.array(first, np.int32))
    qb, kb, vb = lay["gq"] // GLA_KEY, lay["gk"] // GLA_KEY, lay["gv"] // GLA_VAL
    grid_spec = pltpu.PrefetchScalarGridSpec(
        num_scalar_prefetch=2,
        grid=(t // tb,),
        in_specs=[
            pl.BlockSpec((tb, GLA_KEY), lambda j, rbr, fr: (rbr[j], qb)),
            pl.BlockSpec((tb, GLA_KEY), lambda j, rbr, fr: (rbr[j], kb)),
            pl.BlockSpec((tb, GLA_VAL), lambda j, rbr, fr: (rbr[j], vb)),
            pl.BlockSpec((tb, LANES), lambda j, rbr, fr: (rbr[j], 0)),
            pl.BlockSpec((GLA_HEADS, LANES, GLA_DK), lambda j, rbr, fr: (0, 0, 0)),
            pl.BlockSpec((1, GLA_KEY), lambda j, rbr, fr: (0, 0)),
        ],
        out_specs=pl.BlockSpec((tb, GLA_VAL), lambda j, rbr, fr: (rbr[j], 0)),
        scratch_shapes=[pltpu.VMEM((GLA_HEADS, GLA_DV, GLA_DK), F32)],
    )
    return pl.pallas_call(
        functools.partial(_gla_kernel, reverse=reverse),
        grid_spec=grid_spec,
        out_shape=jax.ShapeDtypeStruct((t, GLA_VAL), BF16),
        compiler_params=_cparams(("arbitrary",)),
        name="gla_bwd" if reverse else "gla_fwd",
    )(rb, first, proj, proj, proj, ga, a2p, ab)


FLASH_TQ = 256
FLASH_TK = 512
LOG2E = 1.4426950408889634


def _qkprep_kernel(pb_ref, aq_ref, ak_ref, av_ref, cos_ref, sin_ref, gq_ref, gk_ref, qt_ref, ka_ref, vt_ref):
    cosv = cos_ref[...]
    sinv = sin_ref[...]
    scale = HEAD_DIM ** -0.5 * LOG2E
    nsub = FLASH_TK // FLASH_TQ

    def one(x, g):
        xn = _rms(x.astype(F32), g)
        return xn * cosv + pltpu.roll(xn, HEAD_DIM // 2, 1) * sinv

    for h in range(GQA_HEADS):
        sl = slice(h * HEAD_DIM, (h + 1) * HEAD_DIM)
        qt = (one(aq_ref[:, sl], gq_ref[...]) * scale).T.astype(BF16)
        for u in range(nsub):
            qt_ref[u, sl, :] = qt[:, u * FLASH_TQ:(u + 1) * FLASH_TQ]
    for h in range(GQA_KV_HEADS):
        sl = slice(h * HEAD_DIM, (h + 1) * HEAD_DIM)
        ka_ref[:, sl] = one(ak_ref[:, sl], gk_ref[...]).astype(BF16)
        vt_ref[0, sl, :] = av_ref[:, sl].astype(F32).T.astype(BF16)


def _qkprep(proj, cos_t, sin_t, gq, gk, lay, seqs):
    t = proj.shape[0]
    tm = FLASH_TK
    nsub = FLASH_TK // FLASH_TQ
    pb = []
    for (start, length) in seqs:
        pb += list(range(length // tm))
    pb = jnp.asarray(np.array(pb, np.int32))
    aqb, akb = lay["aq"] // GQA_Q, lay["ak"] // GQA_KV
    grid_spec = pltpu.PrefetchScalarGridSpec(
        num_scalar_prefetch=1,
        grid=(t // tm,),
        in_specs=[
            pl.BlockSpec((tm, GQA_Q), lambda i, p: (i, aqb)),
            pl.BlockSpec((tm, GQA_KV), lambda i, p: (i, akb)),
            pl.BlockSpec((tm, GQA_KV), lambda i, p: (i, akb + 1)),
            pl.BlockSpec((tm, HEAD_DIM), lambda i, p: (p[i], 0)),
            pl.BlockSpec((tm, HEAD_DIM), lambda i, p: (p[i], 0)),
            pl.BlockSpec((1, HEAD_DIM), lambda i, p: (0, 0)),
            pl.BlockSpec((1, HEAD_DIM), lambda i, p: (0, 0)),
        ],
        out_specs=[
            pl.BlockSpec((nsub, GQA_Q, FLASH_TQ), lambda i, p: (i, 0, 0)),
            pl.BlockSpec((tm, GQA_KV), lambda i, p: (i, 0)),
            pl.BlockSpec((1, GQA_KV, tm), lambda i, p: (i, 0, 0)),
        ],
    )
    return pl.pallas_call(
        _qkprep_kernel,
        grid_spec=grid_spec,
        out_shape=[jax.ShapeDtypeStruct((t // FLASH_TQ, GQA_Q, FLASH_TQ), BF16),
                   jax.ShapeDtypeStruct((t, GQA_KV), BF16),
                   jax.ShapeDtypeStruct((t // tm, GQA_KV, tm), BF16)],
        compiler_params=_cparams(("arbitrary",)),
        name="qkprep",
    )(pb, proj, proj, proj, cos_t, sin_t, gq, gk)


def _flash_kernel(q_ref, k_ref, v_ref, prev_ref, o_ref, acc_ref, *, nk):
    del prev_ref
    tq, tk = FLASH_TQ, FLASH_TK
    nq = GQA_GROUP * tq
    qt = jnp.concatenate([q_ref[0, g * HEAD_DIM:(g + 1) * HEAD_DIM, :] for g in range(GQA_GROUP)], axis=1)
    acc_ref[...] = jnp.zeros(acc_ref.shape, F32)

    def step(ci, carry):
        m_prev, l_prev = carry
        off = pl.multiple_of(ci * tk, tk)
        st = _dot(k_ref[pl.ds(off, tk), :], qt)
        m_new = jnp.maximum(m_prev, jnp.max(st, axis=0, keepdims=True))
        alpha = jnp.exp2(m_prev - m_new)
        pt = jnp.exp2(st - m_new)
        l_new = alpha * l_prev + jnp.sum(pt, axis=0, keepdims=True)
        acc_ref[...] = alpha * acc_ref[...] + _dot(v_ref[ci], pt.astype(BF16))
        return m_new, l_new

    m0 = jnp.full((1, nq), -jnp.inf, F32)
    l0 = jnp.zeros((1, nq), F32)
    _, l_fin = lax.fori_loop(0, nk, step, (m0, l0), unroll=2 if nk % 2 == 0 else 1)
    out_t = acc_ref[...] / l_fin
    for g in range(GQA_GROUP):
        o_ref[:, g * HEAD_DIM:(g + 1) * HEAD_DIM] = out_t[:, g * tq:(g + 1) * tq].T.astype(BF16)


def _flash_group(qt3, ka, vt3, start, nseq, length, prev):
    t = ka.shape[0]
    tq, tk = FLASH_TQ, FLASH_TK
    nqt = length // tq
    nk = length // tk
    qb0 = start // tq
    sb0 = start // length
    gw = GQA_GROUP * HEAD_DIM
    in_specs = [
        pl.BlockSpec((1, gw, tq), lambda s, kv, i: (qb0 + s * nqt + i, kv, 0)),
        pl.BlockSpec((length, HEAD_DIM), lambda s, kv, i: (sb0 + s, kv)),
        pl.BlockSpec((nk, HEAD_DIM, tk), lambda s, kv, i: (sb0 + s, kv, 0)),
        pl.BlockSpec(memory_space=pl.ANY),
    ]
    return pl.pallas_call(
        functools.partial(_flash_kernel, nk=nk),
        grid=(nseq, GQA_KV_HEADS, nqt),
        in_specs=in_specs,
        out_specs=pl.BlockSpec((tq, gw), lambda s, kv, i: (qb0 + s * nqt + i, kv)),
        out_shape=jax.ShapeDtypeStruct((t, GQA_Q), BF16),
        scratch_shapes=[pltpu.VMEM((HEAD_DIM, GQA_GROUP * tq), F32)],
        input_output_aliases={3: 0},
        compiler_params=_cparams(("arbitrary", "arbitrary", "arbitrary")),
        name="gqa_flash",
    )(qt3, ka, vt3, prev)


def _memkv_kernel(mem_ref, g_ref, w_ref, gk_ref, km_ref, vm_ref):
    hb = _rms(mem_ref[...], g_ref[...]).astype(BF16)
    kv = _dot(hb, w_ref[...])
    for h in range(MEM_HEADS):
        sl = slice(h * MEM_HD, (h + 1) * MEM_HD)
        km_ref[:, sl] = _rms(kv[:, sl], gk_ref[...]).astype(BF16)
    vm_ref[...] = kv[:, MEM_W:].astype(BF16)


def _memkv(mem, g, w, gk):
    rows, d = mem.shape
    m = 256
    return pl.pallas_call(
        _memkv_kernel,
        grid=(rows // m,),
        in_specs=[
            pl.BlockSpec((m, d), lambda i: (i, 0)),
            pl.BlockSpec((1, d), lambda i: (0, 0)),
            pl.BlockSpec((d, 2 * MEM_W), lambda i: (0, 0)),
            pl.BlockSpec((1, MEM_HD), lambda i: (0, 0)),
        ],
        out_specs=[pl.BlockSpec((m, MEM_W), lambda i: (i, 0)), pl.BlockSpec((m, MEM_W), lambda i: (i, 0))],
        out_shape=[jax.ShapeDtypeStruct((rows, MEM_W), BF16), jax.ShapeDtypeStruct((rows, MEM_W), BF16)],
        compiler_params=_cparams(("arbitrary",)),
        name="memkv",
    )(mem, g, w, gk)


def _memattn_kernel(sq_ref, mq_ref, km_ref, vm_ref, gq_ref, o_ref):
    scale = MEM_HD ** -0.5
    for h in range(MEM_HEADS):
        sl = slice(h * MEM_HD, (h + 1) * MEM_HD)
        qn = (_rms(mq_ref[:, sl].astype(F32), gq_ref[...]) * scale).astype(BF16)
        s = _dot_nt(qn, km_ref[:, sl])
        m = jnp.max(s, axis=1, keepdims=True)
        p = jnp.exp(s - m)
        l = jnp.sum(p, axis=1, keepdims=True)
        o_ref[:, sl] = (_dot(p.astype(BF16), vm_ref[:, sl]) / l).astype(BF16)


def _memattn(proj, km, vm, gq, lay, seqs, n_mem, tm=512):
    t = proj.shape[0]
    sq = []
    for si, (start, length) in enumerate(seqs):
        sq += [si] * (length // tm)
    sq = jnp.asarray(np.array(sq, np.int32))
    mqb = lay["mq"] // MEM_W
    grid_spec = pltpu.PrefetchScalarGridSpec(
        num_scalar_prefetch=1,
        grid=(t // tm,),
        in_specs=[
            pl.BlockSpec((tm, MEM_W), lambda i, s: (i, mqb)),
            pl.BlockSpec((n_mem, MEM_W), lambda i, s: (s[i], 0)),
            pl.BlockSpec((n_mem, MEM_W), lambda i, s: (s[i], 0)),
            pl.BlockSpec((1, MEM_HD), lambda i, s: (0, 0)),
        ],
        out_specs=pl.BlockSpec((tm, MEM_W), lambda i, s: (i, 0)),
    )
    return pl.pallas_call(
        _memattn_kernel,
        grid_spec=grid_spec,
        out_shape=jax.ShapeDtypeStruct((t, MEM_W), BF16),
        compiler_params=_cparams(("arbitrary",)),
        name="memattn",
    )(sq, proj, km, vm, gq)


def _merge_kernel(of_ref, ob_ref, gg_ref, og_ref, om_ref, mg_ref, ggla_ref, w0_ref, w1_ref, w2_ref, o_ref, *, d):
    o = of_ref[...].astype(F32) + ob_ref[...].astype(F32)
    gg = gg_ref[...].astype(F32)
    parts = []
    for h in range(GLA_HEADS):
        sl = slice(h * GLA_DV, (h + 1) * GLA_DV)
        gh = gg[:, sl]
        parts.append((_rms(o[:, sl], ggla_ref[...]) * (gh * _sigmoid(gh))).astype(BF16))
    y0 = jnp.concatenate(parts, axis=1)
    acc = _sigmoid(mg_ref[:, 0:d].astype(F32)) * _dot(y0, w0_ref[...])
    acc = acc + _sigmoid(mg_ref[:, d:2 * d].astype(F32)) * _dot(og_ref[...], w1_ref[...])
    acc = acc + _sigmoid(mg_ref[:, 2 * d:3 * d].astype(F32)) * _dot(om_ref[...], w2_ref[...])
    o_ref[...] = acc.astype(BF16)


def _const_spec(shape):
    nd = len(shape)
    return pl.BlockSpec(shape, lambda i: (0,) * nd, pipeline_mode=pl.Buffered(1))


def _merge(o_f, o_b, proj, o_gqa, o_mem, g_gla, w0, w1, w2, lay, d, tm=512):
    t = proj.shape[0]
    ggb = lay["gg"] // GLA_VAL
    return pl.pallas_call(
        functools.partial(_merge_kernel, d=d),
        grid=(t // tm,),
        in_specs=[
            pl.BlockSpec((tm, GLA_VAL), lambda i: (i, 0)),
            pl.BlockSpec((tm, GLA_VAL), lambda i: (i, 0)),
            pl.BlockSpec((tm, GLA_VAL), lambda i: (i, ggb)),
            pl.BlockSpec((tm, GQA_Q), lambda i: (i, 0)),
            pl.BlockSpec((tm, MEM_W), lambda i: (i, 0)),
            pl.BlockSpec((tm, 3 * d), lambda i: (i, 0)),
            _const_spec((1, GLA_DV)),
            _const_spec((GLA_VAL, d)),
            _const_spec((GQA_Q, d)),
            _const_spec((MEM_W, d)),
        ],
        out_specs=pl.BlockSpec((tm, d), lambda i: (i, 0)),
        out_shape=jax.ShapeDtypeStruct((t, d), BF16),
        compiler_params=_cparams(("arbitrary",)),
        name="merge",
    )(o_f, o_b, proj, o_gqa, o_mem, proj, g_gla, w0, w1, w2)


ROUTER_ROWS = 40


def _outproj_kernel(mrg_ref, xp_ref, xs_ref, wo_ref, gf_ref, wrh_ref, wrl_ref, br_ref,
                    x1_ref, h2p_ref, lg_ref, *, npb, d):
    i = pl.program_id(0)
    x = jnp.where(i < npb, xp_ref[...], xs_ref[...])
    x1 = x + _dot(mrg_ref[...], wo_ref[...])
    x1_ref[...] = x1
    h2 = _rms(x1, gf_ref[...])
    h_hi = h2.astype(BF16)
    h_lo = (h2 - h_hi.astype(F32)).astype(BF16)
    lg = _dot_nt(wrh_ref[...], h_hi) + _dot_nt(wrh_ref[...], h_lo) + _dot_nt(wrl_ref[...], h_hi)
    lg_ref[...] = lg + br_ref[:, 0:1]
    half = d // 2
    h2p_ref[...] = _pack_pair(h2[:, :half], h2[:, half:])


def _outproj(merged, xp, xs, wo, gf, wr_hi, wr_lo, br, tm=512):
    tp, d = xp.shape
    t = merged.shape[0]
    npb = tp // tm
    return pl.pallas_call(
        functools.partial(_outproj_kernel, npb=npb, d=d),
        grid=(t // tm,),
        in_specs=[
            pl.BlockSpec((tm, d), lambda i: (i, 0)),
            pl.BlockSpec((tm, d), lambda i: (jnp.minimum(i, npb - 1), 0)),
            pl.BlockSpec((tm, d), lambda i: (jnp.maximum(i - npb, 0), 0)),
            _const_spec((d, d)),
            _const_spec((1, d)),
            _const_spec((ROUTER_ROWS, d)),
            _const_spec((ROUTER_ROWS, d)),
            _const_spec((ROUTER_ROWS, LANES)),
        ],
        out_specs=[
            pl.BlockSpec((tm, d), lambda i: (i, 0)),
            pl.BlockSpec((tm, d // 2), lambda i: (i, 0)),
            pl.BlockSpec((ROUTER_ROWS, tm), lambda i: (0, i)),
        ],
        out_shape=[jax.ShapeDtypeStruct((t, d), F32), jax.ShapeDtypeStruct((t, d // 2), jnp.uint32),
                   jax.ShapeDtypeStruct((ROUTER_ROWS, t), F32)],
        compiler_params=_cparams(("arbitrary",)),
        name="outproj",
    )(merged, xp, xs, wo, gf, wr_hi, wr_lo, br)


def _route_kernel(lg_ref, info_ref, w_ref, cnt_ref, carry_ref, *, tm):
    i = pl.program_id(0)

    @pl.when(i == 0)
    def _():
        carry_ref[...] = jnp.zeros_like(carry_ref)

    neg = -jnp.inf
    row8 = lax.broadcasted_iota(jnp.int32, (8, tm), 0)
    g = jnp.where(row8 < N_GROUPS, lg_ref[0:8, :], neg)
    gmax = jnp.max(g, axis=0, keepdims=True)
    gsum = jnp.sum(jnp.exp(g - gmax), axis=0, keepdims=True)
    p_grp = 1.0 / gsum
    grp = jnp.min(jnp.where(g == gmax, row8, 8), axis=0, keepdims=True)

    e8 = lg_ref[8:16, :]
    for gi in range(1, N_GROUPS):
        e8 = jnp.where(grp == gi, lg_ref[8 + 8 * gi:16 + 8 * gi, :], e8)
    m1 = jnp.max(e8, axis=0, keepdims=True)
    i1 = jnp.min(jnp.where(e8 == m1, row8, 8), axis=0, keepdims=True)
    e8b = jnp.where(row8 == i1, neg, e8)
    m2 = jnp.max(e8b, axis=0, keepdims=True)
    i2 = jnp.min(jnp.where(e8b == m2, row8, 8), axis=0, keepdims=True)
    zsum = jnp.sum(jnp.exp(e8 - m1), axis=0, keepdims=True)
    p1 = 1.0 / zsum
    p2 = jnp.exp(m2 - m1) / zsum
    den = p1 + p2
    w1 = p_grp * p1 / den
    w2 = p_grp * p2 / den
    eid1 = grp * EXPERTS_PER_GROUP + i1
    eid2 = grp * EXPERTS_PER_GROUP + i2

    rowe = lax.broadcasted_iota(jnp.int32, (N_EXPERTS, tm), 0)
    oh1 = (rowe == eid1)
    oh2 = (rowe == eid2)
    tr = lax.broadcasted_iota(jnp.int32, (tm, tm), 0)
    tc = lax.broadcasted_iota(jnp.int32, (tm, tm), 1)
    upper = (tr < tc).astype(BF16)
    cum1 = _dot(oh1.astype(BF16), upper)
    cum2 = _dot(oh2.astype(BF16), upper)
    oh1f = oh1.astype(F32)
    oh2f = oh2.astype(F32)
    tot1 = jnp.sum(oh1f, axis=1, keepdims=True)
    tot2 = jnp.sum(oh2f, axis=1, keepdims=True)
    carry = carry_ref[:, 0:1]
    rank1 = jnp.sum(oh1f * (cum1 + carry), axis=0, keepdims=True)
    rank2 = jnp.sum(oh2f * (cum2 + carry + tot1), axis=0, keepdims=True)
    new_carry = carry + tot1 + tot2
    carry_ref[...] = jnp.broadcast_to(new_carry, carry_ref.shape)
    cnt_ref[...] = jnp.broadcast_to(new_carry, cnt_ref.shape)

    zi = jnp.zeros((1, tm), jnp.int32)
    info_ref[...] = jnp.concatenate(
        [eid1, eid2, rank1.astype(jnp.int32), rank2.astype(jnp.int32), zi, zi, zi, zi], axis=0)
    zf = jnp.zeros((1, tm), F32)
    w_ref[...] = jnp.concatenate([w1, w2, zf, zf, zf, zf, zf, zf], axis=0)


def _route(logits_t, tm=512):
    t = logits_t.shape[1]
    return pl.pallas_call(
        functools.partial(_route_kernel, tm=tm),
        grid=(t // tm,),
        in_specs=[pl.BlockSpec((ROUTER_ROWS, tm), lambda i: (0, i))],
        out_specs=[
            pl.BlockSpec((8, tm), lambda i: (0, i)),
            pl.BlockSpec((8, tm), lambda i: (0, i)),
            pl.BlockSpec((N_EXPERTS, LANES), lambda i: (0, 0)),
        ],
        out_shape=[jax.ShapeDtypeStruct((8, t), jnp.int32), jax.ShapeDtypeStruct((8, t), F32),
                   jax.ShapeDtypeStruct((N_EXPERTS, LANES), F32)],
        scratch_shapes=[pltpu.VMEM((N_EXPERTS, LANES), F32)],
        compiler_params=_cparams(("arbitrary",)),
        name="route",
    )(logits_t)


def _dispatch_kernel(dest_ref, h_ref, xs_in_ref, xs_ref, sem, *, tm):
    del xs_in_ref

    def row_copy(t, d):
        return pltpu.make_async_copy(h_ref.at[pl.ds(t, 1)], xs_ref.at[pl.ds(d, 1)], sem)

    def issue(t, carry):
        row_copy(t, dest_ref[0, t]).start()
        row_copy(t, dest_ref[1, t]).start()
        return carry

    lax.fori_loop(0, tm, issue, 0, unroll=4)
    tile_wait = pltpu.make_async_copy(h_ref, xs_ref.at[pl.ds(0, tm)], sem)
    tile_wait.wait()
    tile_wait.wait()


def _dispatch(dest, h2p, xs_init, tm=512):
    t, hw = h2p.shape
    return pl.pallas_call(
        functools.partial(_dispatch_kernel, tm=tm),
        grid=(t // tm,),
        in_specs=[
            pl.BlockSpec((8, tm), lambda i: (0, i), memory_space=pltpu.SMEM),
            pl.BlockSpec((tm, hw), lambda i: (i, 0)),
            pl.BlockSpec(memory_space=pl.ANY),
        ],
        out_specs=pl.BlockSpec(memory_space=pl.ANY),
        out_shape=jax.ShapeDtypeStruct(xs_init.shape, xs_init.dtype),
        scratch_shapes=[pltpu.SemaphoreType.DMA(())],
        input_output_aliases={2: 0},
        compiler_params=_cparams(("arbitrary",)),
        name="dispatch",
    )(dest, h2p, xs_init)


MOE_BLK = 256


def _expert_kernel(be_ref, nu_ref, x_ref, wg_ref, wu_ref, wd_ref, y_ref, *, d):
    b = pl.program_id(0)

    @pl.when(b < nu_ref[0])
    def _():
        lo, hi = _unpack_pair(x_ref[...])
        x = jnp.concatenate([lo, hi], axis=1).astype(BF16)
        gate = _dot(x, wg_ref[0])
        up = _dot(x, wu_ref[0])
        hid = (gate * _sigmoid(gate) * up).astype(BF16)
        y = _dot(hid, wd_ref[0])
        half = d // 2
        y_ref[...] = _pack_pair(y[:, :half], y[:, half:])

    @pl.when(b >= nu_ref[0])
    def _():
        y_ref[...] = jnp.zeros_like(y_ref)


def _experts(blk_expert, n_used, xs, wg, wu, wd, d):
    p, hw = xs.shape
    nb = p // MOE_BLK

    def xmap(b, be, nu):
        return (jnp.minimum(b, nu[0] - 1), 0)

    grid_spec = pltpu.PrefetchScalarGridSpec(
        num_scalar_prefetch=2,
        grid=(nb,),
        in_specs=[
            pl.BlockSpec((MOE_BLK, hw), xmap),
            pl.BlockSpec((1, d, D_EXPERT), lambda b, be, nu: (be[b], 0, 0)),
            pl.BlockSpec((1, d, D_EXPERT), lambda b, be, nu: (be[b], 0, 0)),
            pl.BlockSpec((1, D_EXPERT, d), lambda b, be, nu: (be[b], 0, 0)),
        ],
        out_specs=pl.BlockSpec((MOE_BLK, hw), lambda b, be, nu: (b, 0)),
    )
    return pl.pallas_call(
        functools.partial(_expert_kernel, d=d),
        grid_spec=grid_spec,
        out_shape=jax.ShapeDtypeStruct((p, hw), jnp.uint32),
        compiler_params=_cparams(("arbitrary",)),
        name="experts",
    )(blk_expert, n_used, xs, wg, wu, wd)


def _combine_kernel(dest_ref, x1_ref, wt_ref, yb_ref, op_ref, os_ref, buf_ref, sem, *, tm, npb):
    i = pl.program_id(0)

    def row_copy(k, t, d):
        return pltpu.make_async_copy(yb_ref.at[pl.ds(d, 1)], buf_ref.at[k, pl.ds(t, 1)], sem)

    def issue(t, carry):
        row_copy(0, t, dest_ref[0, t]).start()
        row_copy(1, t, dest_ref[1, t]).start()
        return carry

    lax.fori_loop(0, tm, issue, 0, unroll=4)
    for k in range(TOP_K):
        pltpu.make_async_copy(yb_ref.at[pl.ds(0, tm)], buf_ref.at[k], sem).wait()

    w = wt_ref[...]
    a_lo, a_hi = _unpack_pair(buf_ref[0])
    b_lo, b_hi = _unpack_pair(buf_ref[1])
    w1 = w[:, 0:1]
    w2 = w[:, 1:2]
    moe = jnp.concatenate([a_lo * w1 + b_lo * w2, a_hi * w1 + b_hi * w2], axis=1)
    out = x1_ref[...] + moe

    @pl.when(i < npb)
    def _():
        op_ref[...] = out

    @pl.when(i >= npb)
    def _():
        os_ref[...] = out


def _combine(dest, x1, wt, yb, tp, tm=256):
    t, d = x1.shape
    hw = yb.shape[1]
    npb = tp // tm
    return pl.pallas_call(
        functools.partial(_combine_kernel, tm=tm, npb=npb),
        grid=(t // tm,),
        in_specs=[
            pl.BlockSpec((8, tm), lambda i: (0, i), memory_space=pltpu.SMEM),
            pl.BlockSpec((tm, d), lambda i: (i, 0)),
            pl.BlockSpec((tm, 8), lambda i: (i, 0)),
            pl.BlockSpec(memory_space=pl.ANY),
        ],
        out_specs=[
            pl.BlockSpec((tm, d), lambda i: (jnp.minimum(i, npb - 1), 0)),
            pl.BlockSpec((tm, d), lambda i: (jnp.maximum(i - npb, 0), 0)),
        ],
        out_shape=[jax.ShapeDtypeStruct((tp, d), F32), jax.ShapeDtypeStruct((t - tp, d), F32)],
        scratch_shapes=[pltpu.VMEM((2, tm, hw), jnp.uint32), pltpu.SemaphoreType.DMA(())],
        compiler_params=_cparams(("arbitrary",)),
        name="combine",
    )(dest, x1, wt, yb)


def _rope_tables(length):
    rows = length // GRID_W
    row = jnp.broadcast_to(jnp.arange(rows, dtype=F32)[:, None], (rows, GRID_W)).reshape(length)
    col = jnp.broadcast_to(jnp.arange(GRID_W, dtype=F32)[None, :], (rows, GRID_W)).reshape(length)
    axis_dim = HEAD_DIM // 2
    freqs = ROPE_THETA ** (-jnp.arange(0, axis_dim, 2, dtype=F32) / axis_dim)
    ang = jnp.concatenate([row[:, None] * freqs, col[:, None] * freqs], axis=-1)
    c, s = jnp.cos(ang), jnp.sin(ang)
    return jnp.concatenate([c, c], axis=1), jnp.concatenate([-s, s], axis=1)


def _layer(xp, xs, mem_all, seqs, n_mem, p):
    d = xp.shape[1]
    tp = xp.shape[0]
    lay = _proj_layout(d)

    w_in = p["w_in"]
    sizes = (GLA_KEY, GLA_KEY, GLA_VAL, GLA_VAL, GLA_RANK, GLA_RANK, GQA_Q, GQA_KV, GQA_KV, MEM_W, 3 * d)
    offs = np.concatenate([[0], np.cumsum(sizes)])
    seg = {n: w_in[:, offs[k]:offs[k + 1]] for k, n in enumerate(
        ("gq", "gk", "gv", "gg", "ga_f", "ga_b", "aq", "ak", "av", "mq", "mg"))}
    def deinterleave(w, heads):
        rows = w.shape[0]
        return w.reshape(rows, heads, HEAD_DIM // 2, 2).transpose(0, 1, 3, 2).reshape(rows, heads * HEAD_DIM)

    w_main = jnp.concatenate([seg["mg"], seg["gv"], seg["gg"], deinterleave(seg["aq"], GQA_HEADS), seg["mq"],
                              seg["gq"], seg["gk"], deinterleave(seg["ak"], GQA_KV_HEADS), seg["av"]],
                             axis=1).astype(BF16)
    w_ga = jnp.concatenate([seg["ga_f"], seg["ga_b"], jnp.zeros((d, LANES - 2 * GLA_RANK), F32)],
                           axis=1).astype(BF16)

    def a2_pad(a2, row0):
        z = jnp.zeros((LANES, GLA_KEY), F32).at[row0:row0 + GLA_RANK].set(a2)
        return z.reshape(LANES, GLA_HEADS, GLA_DK).transpose(1, 0, 2).astype(BF16)

    gq_perm = deinterleave(p["g_q_gqa"][None, :], 1)
    gk_perm = deinterleave(p["g_k_gqa"][None, :], 1)

    proj, ga = _inproj(xp, xs, p["g_mix"][None, :], w_main, w_ga)

    o_f = _gla(proj, ga, a2_pad(p["gla_a2_fwd"], 0), p["gla_ab_fwd"][None, :], lay, seqs, False)
    o_b = _gla(proj, ga, a2_pad(p["gla_a2_bwd"], GLA_RANK), p["gla_ab_bwd"][None, :], lay, seqs, True)

    max_len = max(length for _, length in seqs)
    cos_t, sin_t = _rope_tables(max_len)
    qt3, ka, vt3 = _qkprep(proj, cos_t, sin_t, gq_perm, gk_perm, lay, seqs)
    groups = []
    for (start, length) in seqs:
        if groups and groups[-1][2] == length and groups[-1][0] + groups[-1][1] * length == start:
            groups[-1][1] += 1
        else:
            groups.append([start, 1, length])
    o_gqa = jnp.zeros((proj.shape[0], GQA_Q), BF16)
    for (start, nseq, length) in groups:
        o_gqa = _flash_group(qt3, ka, vt3, start, nseq, length, o_gqa)

    km, vm = _memkv(mem_all, p["g_mem_norm"][None, :], p["w_mem_kv"].astype(BF16), p["g_k_mem"][None, :])
    o_mem = _memattn(proj, km, vm, p["g_q_mem"][None, :], lay, seqs, n_mem)

    merged = _merge(o_f, o_b, proj, o_gqa, o_mem, p["g_gla_out"][None, :], p["w_br_gla"].astype(BF16),
                    p["w_br_gqa"].astype(BF16), p["w_br_mem"].astype(BF16), lay, d)
    wr = jnp.zeros((ROUTER_ROWS, d), F32)
    wr = wr.at[0:N_GROUPS].set(p["w_router_group"].T).at[8:8 + N_EXPERTS].set(p["w_router_expert"].T)
    wr_hi = wr.astype(BF16)
    wr_lo = (wr - wr_hi.astype(F32)).astype(BF16)
    br = jnp.zeros((ROUTER_ROWS,), F32).at[0:N_GROUPS].set(p["b_router_group"]).at[8:8 + N_EXPERTS].set(
        p["b_router_expert"])
    br = jnp.broadcast_to(br[:, None], (ROUTER_ROWS, LANES))
    x1, h2p, logits_t = _outproj(merged, xp, xs, p["w_out"].astype(BF16), p["g_ffn"][None, :], wr_hi, wr_lo, br)

    info, wts, counts = _route(logits_t)
    t = x1.shape[0]
    cnt = counts[:, 0].astype(jnp.int32)
    padded = ((cnt + MOE_BLK - 1) // MOE_BLK) * MOE_BLK
    pad_ends = jnp.cumsum(padded)
    pad_starts = pad_ends - padded
    nb = (t * TOP_K) // MOE_BLK + N_EXPERTS
    n_used = (pad_ends[-1] // MOE_BLK).astype(jnp.int32)
    blk = jnp.arange(nb, dtype=jnp.int32)
    blk_expert = jnp.searchsorted(pad_ends, jnp.minimum(blk, n_used - 1) * MOE_BLK, side="right")
    blk_expert = jnp.clip(blk_expert, 0, N_EXPERTS - 1).astype(jnp.int32)
    dest = pad_starts[info[0:2]] + info[2:4]
    dest = jnp.concatenate([dest, jnp.zeros((6, t), jnp.int32)], axis=0)

    xs_sorted = _dispatch(dest, h2p, jnp.zeros((nb * MOE_BLK, d // 2), jnp.uint32))
    yb = _experts(blk_expert, n_used[None], xs_sorted, p["w_exp_gate"].astype(BF16),
                  p["w_exp_up"].astype(BF16), p["w_exp_down"].astype(BF16), d)
    return _combine(dest, x1, wts.T, yb, tp)


def kernel(x_prompt, x_sample, mem_prompt, mem_sample, g_mix, w_in, gla_a2_fwd, gla_ab_fwd, gla_a2_bwd, gla_ab_bwd, g_gla_out, g_q_gqa, g_k_gqa, g_mem_norm, w_mem_kv, g_q_mem, g_k_mem, w_br_gla, w_br_gqa, w_br_mem, w_out, g_ffn, w_router_group, b_router_group, w_router_expert, b_router_expert, w_exp_gate, w_exp_up, w_exp_down):
    bp, lp, d = x_prompt.shape
    bs, ls, _ = x_sample.shape
    n_mem = mem_prompt.shape[1]
    depth = g_mix.shape[0]
    seqs = [(b * lp, lp) for b in range(bp)] + [(bp * lp + b * ls, ls) for b in range(bs)]
    xp = x_prompt.reshape(bp * lp, d)
    xs = x_sample.reshape(bs * ls, d)
    mem_all = jnp.concatenate([mem_prompt.reshape(bp * n_mem, d), mem_sample.reshape(bs * n_mem, d)], axis=0)
    names = ("g_mix", "w_in", "gla_a2_fwd", "gla_ab_fwd", "gla_a2_bwd", "gla_ab_bwd", "g_gla_out", "g_q_gqa",
             "g_k_gqa", "g_mem_norm", "w_mem_kv", "g_q_mem", "g_k_mem", "w_br_gla", "w_br_gqa", "w_br_mem",
             "w_out", "g_ffn", "w_router_group", "b_router_group", "w_router_expert", "b_router_expert",
             "w_exp_gate", "w_exp_up", "w_exp_down")
    vals = (g_mix, w_in, gla_a2_fwd, gla_ab_fwd, gla_a2_bwd, gla_ab_bwd, g_gla_out, g_q_gqa, g_k_gqa,
            g_mem_norm, w_mem_kv, g_q_mem, g_k_mem, w_br_gla, w_br_gqa, w_br_mem, w_out, g_ffn,
            w_router_group, b_router_group, w_router_expert, b_router_expert, w_exp_gate, w_exp_up, w_exp_down)
    for layer in range(depth):
        p = {n: v[layer] for n, v in zip(names, vals)}
        xp, xs = _layer(xp, xs, mem_all, seqs, n_mem, p)
    return xp.reshape(bp, lp, d), xs.reshape(bs, ls, d)
```

```python
import functools

import numpy as np
import jax
import jax.numpy as jnp
from jax import lax
from jax.experimental import pallas as pl
from jax.experimental.pallas import tpu as pltpu

F32 = jnp.float32
BF16 = jnp.bfloat16

GRID_W = 64
GLA_HEADS, GLA_DK, GLA_DV = 4, 128, 256
GLA_KEY, GLA_VAL = GLA_HEADS * GLA_DK, GLA_HEADS * GLA_DV
GLA_RANK, GLA_TAU, GLA_CHUNK = 16, 16.0, 64
GQA_HEADS, GQA_KV_HEADS, HEAD_DIM = 8, 2, 128
GQA_GROUP = GQA_HEADS // GQA_KV_HEADS
GQA_Q, GQA_KV = GQA_HEADS * HEAD_DIM, GQA_KV_HEADS * HEAD_DIM
ROPE_THETA = 10000.0
MEM_HEADS, MEM_HD = 4, 256
MEM_W = MEM_HEADS * MEM_HD
N_GROUPS, EXPERTS_PER_GROUP, TOP_K, D_EXPERT = 4, 8, 2, 512
N_EXPERTS = N_GROUPS * EXPERTS_PER_GROUP
EPS = 1e-6

LANES = 128
VMEM_LIMIT_BYTES = 56 * 1024 * 1024


def _proj_layout(d):
    off = {}
    c = 0
    for name, w in (("mg", 3 * d), ("gv", GLA_VAL), ("gg", GLA_VAL), ("aq", GQA_Q), ("mq", MEM_W),
                    ("gq", GLA_KEY), ("gk", GLA_KEY), ("ak", GQA_KV), ("av", GQA_KV)):
        off[name] = c
        c += w
    off["total"] = c
    return off


def _cparams(sem, vmem=VMEM_LIMIT_BYTES):
    return pltpu.CompilerParams(dimension_semantics=sem, vmem_limit_bytes=vmem)


def _rms(x, g):
    ms = jnp.mean(x * x, axis=-1, keepdims=True)
    return x * lax.rsqrt(ms + EPS) * g


def _dot(a, b):
    return jnp.dot(a, b, preferred_element_type=F32)


def _dot_nt(a, b):
    return lax.dot_general(a, b, (((1,), (1,)), ((), ())), preferred_element_type=F32)


def _dot_tn(a, b):
    return lax.dot_general(a, b, (((0,), (0,)), ((), ())), preferred_element_type=F32)


def _sigmoid(x):
    return 1.0 / (1.0 + jnp.exp(-x))


def _pack_pair(lo, hi):
    lo_b = pltpu.bitcast(lo.astype(BF16).astype(F32), jnp.uint32)
    hi_b = pltpu.bitcast(hi.astype(BF16).astype(F32), jnp.uint32)
    return (hi_b & jnp.uint32(0xFFFF0000)) | (lo_b >> 16)


def _unpack_pair(w):
    lo = pltpu.bitcast(w << 16, F32)
    hi = pltpu.bitcast(w & jnp.uint32(0xFFFF0000), F32)
    return lo, hi


ROW_TILE = 8


def _store_rows_tiled(ref, packed, start=0):
    n = packed.shape[0]
    for c in range(ROW_TILE):
        ref[pl.ds(start * ROW_TILE + c, n, stride=ROW_TILE), :] = packed[:, c * LANES:(c + 1) * LANES]


def _load_rows_tiled(ref, n, start=0, lead=()):
    return jnp.concatenate(
        [ref[lead + (pl.ds(start * ROW_TILE + c, n, stride=ROW_TILE), slice(None))] for c in range(ROW_TILE)],
        axis=1)


def _inproj_kernel(xp_ref, xs_ref, g_ref, w_ref, wga_ref, o_ref, ga_ref, h_ref, *, npb):
    i = pl.program_id(0)
    j = pl.program_id(1)

    def norm(x_ref):
        hb = _rms(x_ref[...], g_ref[...]).astype(BF16)
        h_ref[...] = hb
        ga_ref[...] = _dot(hb, wga_ref[...])

    @pl.when(jnp.logical_and(j == 0, i < npb))
    def _():
        norm(xp_ref)

    @pl.when(jnp.logical_and(j == 0, i >= npb))
    def _():
        norm(xs_ref)

    o_ref[...] = _dot(h_ref[...], w_ref[...]).astype(BF16)


def _inproj(xp, xs, g, w_main, w_ga, tm=1024, tn=512):
    tp, d = xp.shape
    ts = xs.shape[0]
    t = tp + ts
    nc = w_main.shape[1]
    npb = tp // tm
    grid = (t // tm, nc // tn)
    return pl.pallas_call(
        functools.partial(_inproj_kernel, npb=npb),
        grid=grid,
        in_specs=[
            pl.BlockSpec((tm, d), lambda i, j: (jnp.minimum(i, npb - 1), 0)),
            pl.BlockSpec((tm, d), lambda i, j: (jnp.maximum(i - npb, 0), 0)),
            pl.BlockSpec((1, d), lambda i, j: (0, 0)),
            pl.BlockSpec((d, tn), lambda i, j: (0, j)),
            pl.BlockSpec((d, LANES), lambda i, j: (0, 0)),
        ],
        out_specs=[
            pl.BlockSpec((tm, tn), lambda i, j: (i, j)),
            pl.BlockSpec((tm, LANES), lambda i, j: (i, 0)),
        ],
        out_shape=[jax.ShapeDtypeStruct((t, nc), BF16), jax.ShapeDtypeStruct((t, LANES), F32)],
        scratch_shapes=[pltpu.VMEM((tm, d), BF16)],
        compiler_params=_cparams(("arbitrary", "arbitrary")),
        name="inproj",
    )(xp, xs, g, w_main, w_ga)


GLA_TB = 512


def _gla_kernel(rb_ref, first_ref, q_ref, k_ref, v_ref, ga_ref, a2_ref, ab_ref, o_ref, s_ref, *, reverse):
    j = pl.program_id(0)
    tb, c = GLA_TB, GLA_CHUNK
    nch = tb // c

    @pl.when(first_ref[j] == 1)
    def _():
        s_ref[...] = jnp.zeros_like(s_ref)

    r = lax.broadcasted_iota(jnp.int32, (tb, tb), 0)
    cc = lax.broadcasted_iota(jnp.int32, (tb, tb), 1)
    same = (r >> 6) == (cc >> 6)
    if reverse:
        tri = (cc >= r).astype(BF16)
        amask = jnp.logical_and(same, cc > r)
    else:
        tri = (cc <= r).astype(BF16)
        amask = jnp.logical_and(same, cc <= r)

    z = _dot(ga_ref[...].astype(BF16), a2_ref[...]) + ab_ref[...]
    la = (jnp.minimum(z, 0.0) - jnp.log1p(jnp.exp(-jnp.abs(z)))) * (1.0 / GLA_TAU)
    la_hi = la.astype(BF16)
    la_lo = (la - la_hi.astype(F32)).astype(BF16)
    pre_all = _dot(tri, la_hi) + _dot(tri, la_lo)
    scale = GLA_DK ** -0.5
    for h in range(GLA_HEADS):
        pre = pre_all[:, h * GLA_DK:(h + 1) * GLA_DK]
        bs, tots = [], []
        for ci in range(nch):
            r0 = ci * c
            if reverse:
                base = pre[r0 + c:r0 + c + 1] if ci < nch - 1 else jnp.zeros((1, GLA_DK), F32)
                tot = pre[r0:r0 + 1] - base
            else:
                base = pre[r0 - 1:r0] if ci > 0 else jnp.zeros((1, GLA_DK), F32)
                tot = pre[r0 + c - 1:r0 + c] - base
            bs.append(pre[r0:r0 + c] - base)
            tots.append(tot)
        b = jnp.concatenate(bs, axis=0)
        btot = jnp.concatenate([jnp.broadcast_to(tt, (c, GLA_DK)) for tt in tots], axis=0)

        q = q_ref[:, h * GLA_DK:(h + 1) * GLA_DK].astype(F32) * scale
        k = k_ref[:, h * GLA_DK:(h + 1) * GLA_DK].astype(F32)
        v = v_ref[:, h * GLA_DV:(h + 1) * GLA_DV]
        q_e = (q * jnp.exp(b)).astype(BF16)
        k_e = (k * jnp.exp(-b)).astype(BF16)
        k_d = (k * jnp.exp(btot - b)).astype(BF16)
        att = jnp.where(amask, _dot_nt(q_e, k_e), 0.0).astype(BF16)
        o_intra = _dot(att, v)

        st = s_ref[h]
        outs = [None] * nch
        order = range(nch - 1, -1, -1) if reverse else range(nch)
        for ci in order:
            r0 = ci * c
            o_inter = _dot_nt(q_e[r0:r0 + c], st.astype(BF16))
            outs[ci] = o_intra[r0:r0 + c] + o_inter
            kvt = _dot_tn(v[r0:r0 + c], k_d[r0:r0 + c])
            st = st * jnp.exp(tots[ci]) + kvt
        s_ref[h] = st
        o_ref[:, h * GLA_DV:(h + 1) * GLA_DV] = jnp.concatenate(outs, axis=0).astype(BF16)


def _gla(proj, ga, a2p, ab, lay, seqs, reverse):
    t = proj.shape[0]
    tb = GLA_TB
    rb, first = [], []
    for (start, length) in seqs:
        blocks = list(range(start // tb, (start + length) // tb))
        if reverse:
            blocks = blocks[::-1]
        rb += blocks
        first += [1] + [0] * (len(blocks) - 1)
    rb = jnp.asarray(np.array(rb, np.int32))
    first = jnp.asarray(np.array(first, np.int32))
    qb, kb, vb = lay["gq"] // GLA_KEY, lay["gk"] // GLA_KEY, lay["gv"] // GLA_VAL
    grid_spec = pltpu.PrefetchScalarGridSpec(
        num_scalar_prefetch=2,
        grid=(t // tb,),
        in_specs=[
            pl.BlockSpec((tb, GLA_KEY), lambda j, rbr, fr: (rbr[j], qb)),
            pl.BlockSpec((tb, GLA_KEY), lambda j, rbr, fr: (rbr[j], kb)),
            pl.BlockSpec((tb, GLA_VAL), lambda j, rbr, fr: (rbr[j], vb)),
            pl.BlockSpec((tb, LANES), lambda j, rbr, fr: (rbr[j], 0)),
            pl.BlockSpec((LANES, GLA_KEY), lambda j, rbr, fr: (0, 0)),
            pl.BlockSpec((1, GLA_KEY), lambda j, rbr, fr: (0, 0)),
        ],
        out_specs=pl.BlockSpec((tb, GLA_VAL), lambda j, rbr, fr: (rbr[j], 0)),
        scratch_shapes=[pltpu.VMEM((GLA_HEADS, GLA_DV, GLA_DK), F32)],
    )
    return pl.pallas_call(
        functools.partial(_gla_kernel, reverse=reverse),
        grid_spec=grid_spec,
        out_shape=jax.ShapeDtypeStruct((t, GLA_VAL), BF16),
        compiler_params=_cparams(("arbitrary",)),
        name="gla_bwd" if reverse else "gla_fwd",
    )(rb, first, proj, proj, proj, ga, a2p, ab)


FLASH_TQ = 256
FLASH_TK = 512
LOG2E = 1.4426950408889634
BF16_SUBLANES = 16
VT_ROWS = HEAD_DIM + BF16_SUBLANES


def _qkprep_kernel(pb_ref, aq_ref, ak_ref, av_ref, cos_ref, sin_ref, gq_ref, gk_ref, qt_ref, ka_ref, vt_ref):
    cosv = cos_ref[...]
    sinv = sin_ref[...]
    scale = HEAD_DIM ** -0.5 * LOG2E
    nsub = FLASH_TK // FLASH_TQ

    def one(x, g):
        xn = _rms(x.astype(F32), g)
        return xn * cosv + pltpu.roll(xn, HEAD_DIM // 2, 1) * sinv

    for h in range(GQA_HEADS):
        sl = slice(h * HEAD_DIM, (h + 1) * HEAD_DIM)
        qt = (one(aq_ref[:, sl], gq_ref[...]) * scale).T.astype(BF16)
        for u in range(nsub):
            qt_ref[u, sl, :] = qt[:, u * FLASH_TQ:(u + 1) * FLASH_TQ]
    for h in range(GQA_KV_HEADS):
        sl = slice(h * HEAD_DIM, (h + 1) * HEAD_DIM)
        ka_ref[:, sl] = one(ak_ref[:, sl], gk_ref[...]).astype(BF16)
        vt_ref[0, h * VT_ROWS:h * VT_ROWS + HEAD_DIM, :] = av_ref[:, sl].astype(F32).T.astype(BF16)
        vt_ref[0, h * VT_ROWS + HEAD_DIM:(h + 1) * VT_ROWS, :] = jnp.ones((BF16_SUBLANES, FLASH_TK), BF16)


def _qkprep(proj, cos_t, sin_t, gq, gk, lay, seqs):
    t = proj.shape[0]
    tm = FLASH_TK
    nsub = FLASH_TK // FLASH_TQ
    pb = []
    for (start, length) in seqs:
        pb += list(range(length // tm))
    pb = jnp.asarray(np.array(pb, np.int32))
    aqb, akb = lay["aq"] // GQA_Q, lay["ak"] // GQA_KV
    grid_spec = pltpu.PrefetchScalarGridSpec(
        num_scalar_prefetch=1,
        grid=(t // tm,),
        in_specs=[
            pl.BlockSpec((tm, GQA_Q), lambda i, p: (i, aqb)),
            pl.BlockSpec((tm, GQA_KV), lambda i, p: (i, akb)),
            pl.BlockSpec((tm, GQA_KV), lambda i, p: (i, akb + 1)),
            pl.BlockSpec((tm, HEAD_DIM), lambda i, p: (p[i], 0)),
            pl.BlockSpec((tm, HEAD_DIM), lambda i, p: (p[i], 0)),
            pl.BlockSpec((1, HEAD_DIM), lambda i, p: (0, 0)),
            pl.BlockSpec((1, HEAD_DIM), lambda i, p: (0, 0)),
        ],
        out_specs=[
            pl.BlockSpec((nsub, GQA_Q, FLASH_TQ), lambda i, p: (i, 0, 0)),
            pl.BlockSpec((tm, GQA_KV), lambda i, p: (i, 0)),
            pl.BlockSpec((1, GQA_KV_HEADS * VT_ROWS, tm), lambda i, p: (i, 0, 0)),
        ],
    )
    return pl.pallas_call(
        _qkprep_kernel,
        grid_spec=grid_spec,
        out_shape=[jax.ShapeDtypeStruct((t // FLASH_TQ, GQA_Q, FLASH_TQ), BF16),
                   jax.ShapeDtypeStruct((t, GQA_KV), BF16),
                   jax.ShapeDtypeStruct((t // tm, GQA_KV_HEADS * VT_ROWS, tm), BF16)],
        compiler_params=_cparams(("arbitrary",)),
        name="qkprep",
    )(pb, proj, proj, proj, cos_t, sin_t, gq, gk)


def _flash_kernel(q_ref, k_ref, v_ref, prev_ref, o_ref, acc_ref, *, nk):
    del prev_ref
    tq, tk = FLASH_TQ, FLASH_TK
    nq = GQA_GROUP * tq
    qt = jnp.concatenate([q_ref[0, g * HEAD_DIM:(g + 1) * HEAD_DIM, :] for g in range(GQA_GROUP)], axis=1)
    acc_ref[...] = jnp.zeros(acc_ref.shape, F32)

    def step(ci, m_prev):
        off = pl.multiple_of(ci * tk, tk)
        st = _dot(k_ref[pl.ds(off, tk), :], qt)
        m_new = jnp.maximum(m_prev, jnp.max(st, axis=0, keepdims=True))
        alpha = jnp.exp2(m_prev - m_new)
        pt = jnp.exp2(st - m_new).astype(BF16)
        acc_ref[...] = alpha * acc_ref[...] + _dot(v_ref[ci], pt)
        return m_new

    m0 = jnp.full((1, nq), -jnp.inf, F32)
    lax.fori_loop(0, nk, step, m0, unroll=4 if nk % 4 == 0 else 1)
    out_t = acc_ref[0:HEAD_DIM, :] / acc_ref[HEAD_DIM:HEAD_DIM + 1, :]
    for g in range(GQA_GROUP):
        o_ref[:, g * HEAD_DIM:(g + 1) * HEAD_DIM] = out_t[:, g * tq:(g + 1) * tq].T.astype(BF16)


def _flash_group(qt3, ka, vt3, start, nseq, length, prev):
    t = ka.shape[0]
    tq, tk = FLASH_TQ, FLASH_TK
    nqt = length // tq
    nk = length // tk
    qb0 = start // tq
    sb0 = start // length
    gw = GQA_GROUP * HEAD_DIM
    in_specs = [
        pl.BlockSpec((1, gw, tq), lambda s, kv, i: (qb0 + s * nqt + i, kv, 0)),
        pl.BlockSpec((length, HEAD_DIM), lambda s, kv, i: (sb0 + s, kv)),
        pl.BlockSpec((nk, VT_ROWS, tk), lambda s, kv, i: (sb0 + s, kv, 0)),
        pl.BlockSpec(memory_space=pl.ANY),
    ]
    return pl.pallas_call(
        functools.partial(_flash_kernel, nk=nk),
        grid=(nseq, GQA_KV_HEADS, nqt),
        in_specs=in_specs,
        out_specs=pl.BlockSpec((tq, gw), lambda s, kv, i: (qb0 + s * nqt + i, kv)),
        out_shape=jax.ShapeDtypeStruct((t, GQA_Q), BF16),
        scratch_shapes=[pltpu.VMEM((VT_ROWS, GQA_GROUP * tq), F32)],
        input_output_aliases={3: 0},
        compiler_params=_cparams(("arbitrary", "arbitrary", "arbitrary")),
        name="gqa_flash",
    )(qt3, ka, vt3, prev)


def _memkv_kernel(mem_ref, g_ref, w_ref, gk_ref, km_ref, vm_ref):
    hb = _rms(mem_ref[...], g_ref[...]).astype(BF16)
    kv = _dot(hb, w_ref[...])
    for h in range(MEM_HEADS):
        sl = slice(h * MEM_HD, (h + 1) * MEM_HD)
        km_ref[:, sl] = _rms(kv[:, sl], gk_ref[...]).astype(BF16)
    vm_ref[...] = kv[:, MEM_W:].astype(BF16)


def _memkv(mem, g, w, gk):
    rows, d = mem.shape
    m = 256
    return pl.pallas_call(
        _memkv_kernel,
        grid=(rows // m,),
        in_specs=[
            pl.BlockSpec((m, d), lambda i: (i, 0)),
            pl.BlockSpec((1, d), lambda i: (0, 0)),
            pl.BlockSpec((d, 2 * MEM_W), lambda i: (0, 0)),
            pl.BlockSpec((1, MEM_HD), lambda i: (0, 0)),
        ],
        out_specs=[pl.BlockSpec((m, MEM_W), lambda i: (i, 0)), pl.BlockSpec((m, MEM_W), lambda i: (i, 0))],
        out_shape=[jax.ShapeDtypeStruct((rows, MEM_W), BF16), jax.ShapeDtypeStruct((rows, MEM_W), BF16)],
        compiler_params=_cparams(("arbitrary",)),
        name="memkv",
    )(mem, g, w, gk)


def _memattn_kernel(sq_ref, mq_ref, km_ref, vm_ref, gq_ref, o_ref):
    scale = MEM_HD ** -0.5
    for h in range(MEM_HEADS):
        sl = slice(h * MEM_HD, (h + 1) * MEM_HD)
        qn = (_rms(mq_ref[:, sl].astype(F32), gq_ref[...]) * scale).astype(BF16)
        s = _dot_nt(qn, km_ref[:, sl])
        m = jnp.max(s, axis=1, keepdims=True)
        p = jnp.exp(s - m)
        l = jnp.sum(p, axis=1, keepdims=True)
        o_ref[:, sl] = (_dot(p.astype(BF16), vm_ref[:, sl]) / l).astype(BF16)


def _memattn(proj, km, vm, gq, lay, seqs, n_mem, tm=512):
    t = proj.shape[0]
    sq = []
    for si, (start, length) in enumerate(seqs):
        sq += [si] * (length // tm)
    sq = jnp.asarray(np.array(sq, np.int32))
    mqb = lay["mq"] // MEM_W
    grid_spec = pltpu.PrefetchScalarGridSpec(
        num_scalar_prefetch=1,
        grid=(t // tm,),
        in_specs=[
            pl.BlockSpec((tm, MEM_W), lambda i, s: (i, mqb)),
            pl.BlockSpec((n_mem, MEM_W), lambda i, s: (s[i], 0)),
            pl.BlockSpec((n_mem, MEM_W), lambda i, s: (s[i], 0)),
            pl.BlockSpec((1, MEM_HD), lambda i, s: (0, 0)),
        ],
        out_specs=pl.BlockSpec((tm, MEM_W), lambda i, s: (i, 0)),
    )
    return pl.pallas_call(
        _memattn_kernel,
        grid_spec=grid_spec,
        out_shape=jax.ShapeDtypeStruct((t, MEM_W), BF16),
        compiler_params=_cparams(("arbitrary",)),
        name="memattn",
    )(sq, proj, km, vm, gq)


def _merge_kernel(of_ref, ob_ref, gg_ref, og_ref, om_ref, mg_ref, ggla_ref, w0_ref, w1_ref, w2_ref, o_ref, *, d):
    o = of_ref[...].astype(F32) + ob_ref[...].astype(F32)
    gg = gg_ref[...].astype(F32)
    parts = []
    for h in range(GLA_HEADS):
        sl = slice(h * GLA_DV, (h + 1) * GLA_DV)
        gh = gg[:, sl]
        parts.append((_rms(o[:, sl], ggla_ref[...]) * (gh * _sigmoid(gh))).astype(BF16))
    y0 = jnp.concatenate(parts, axis=1)
    acc = _sigmoid(mg_ref[:, 0:d].astype(F32)) * _dot(y0, w0_ref[...])
    acc = acc + _sigmoid(mg_ref[:, d:2 * d].astype(F32)) * _dot(og_ref[...], w1_ref[...])
    acc = acc + _sigmoid(mg_ref[:, 2 * d:3 * d].astype(F32)) * _dot(om_ref[...], w2_ref[...])
    o_ref[...] = acc.astype(BF16)


def _const_spec(shape):
    nd = len(shape)
    return pl.BlockSpec(shape, lambda i: (0,) * nd, pipeline_mode=pl.Buffered(1))


def _merge(o_f, o_b, proj, o_gqa, o_mem, g_gla, w0, w1, w2, lay, d, tm=512):
    t = proj.shape[0]
    ggb = lay["gg"] // GLA_VAL
    return pl.pallas_call(
        functools.partial(_merge_kernel, d=d),
        grid=(t // tm,),
        in_specs=[
            pl.BlockSpec((tm, GLA_VAL), lambda i: (i, 0)),
            pl.BlockSpec((tm, GLA_VAL), lambda i: (i, 0)),
            pl.BlockSpec((tm, GLA_VAL), lambda i: (i, ggb)),
            pl.BlockSpec((tm, GQA_Q), lambda i: (i, 0)),
            pl.BlockSpec((tm, MEM_W), lambda i: (i, 0)),
            pl.BlockSpec((tm, 3 * d), lambda i: (i, 0)),
            _const_spec((1, GLA_DV)),
            _const_spec((GLA_VAL, d)),
            _const_spec((GQA_Q, d)),
            _const_spec((MEM_W, d)),
        ],
        out_specs=pl.BlockSpec((tm, d), lambda i: (i, 0)),
        out_shape=jax.ShapeDtypeStruct((t, d), BF16),
        compiler_params=_cparams(("arbitrary",)),
        name="merge",
    )(o_f, o_b, proj, o_gqa, o_mem, proj, g_gla, w0, w1, w2)


ROUTER_ROWS = 40


def _outproj_kernel(mrg_ref, xp_ref, xs_ref, wo_ref, gf_ref, wrh_ref, wrl_ref, br_ref,
                    x1_ref, h2p_ref, lg_ref, *, npb, d):
    i = pl.program_id(0)
    x = jnp.where(i < npb, xp_ref[...], xs_ref[...])
    x1 = x + _dot(mrg_ref[...], wo_ref[...])
    x1_ref[...] = x1
    h2 = _rms(x1, gf_ref[...])
    h_hi = h2.astype(BF16)
    h_lo = (h2 - h_hi.astype(F32)).astype(BF16)
    lg = _dot_nt(wrh_ref[...], h_hi) + _dot_nt(wrh_ref[...], h_lo) + _dot_nt(wrl_ref[...], h_hi)
    lg_ref[...] = lg + br_ref[:, 0:1]
    half = d // 2
    _store_rows_tiled(h2p_ref, _pack_pair(h2[:, :half], h2[:, half:]))


def _outproj(merged, xp, xs, wo, gf, wr_hi, wr_lo, br, tm=512):
    tp, d = xp.shape
    t = merged.shape[0]
    npb = tp // tm
    assert d // 2 == ROW_TILE * LANES, "packed token rows must fill exactly one (8, 128) tile"
    return pl.pallas_call(
        functools.partial(_outproj_kernel, npb=npb, d=d),
        grid=(t // tm,),
        in_specs=[
            pl.BlockSpec((tm, d), lambda i: (i, 0)),
            pl.BlockSpec((tm, d), lambda i: (jnp.minimum(i, npb - 1), 0)),
            pl.BlockSpec((tm, d), lambda i: (jnp.maximum(i - npb, 0), 0)),
            _const_spec((d, d)),
            _const_spec((1, d)),
            _const_spec((ROUTER_ROWS, d)),
            _const_spec((ROUTER_ROWS, d)),
            _const_spec((ROUTER_ROWS, LANES)),
        ],
        out_specs=[
            pl.BlockSpec((tm, d), lambda i: (i, 0)),
            pl.BlockSpec((tm * ROW_TILE, LANES), lambda i: (i, 0)),
            pl.BlockSpec((ROUTER_ROWS, tm), lambda i: (0, i)),
        ],
        out_shape=[jax.ShapeDtypeStruct((t, d), F32), jax.ShapeDtypeStruct((t * ROW_TILE, LANES), jnp.uint32),
                   jax.ShapeDtypeStruct((ROUTER_ROWS, t), F32)],
        compiler_params=_cparams(("arbitrary",)),
        name="outproj",
    )(merged, xp, xs, wo, gf, wr_hi, wr_lo, br)


def _route_kernel(lg_ref, info_ref, w_ref, cnt_ref, carry_ref, *, tm):
    i = pl.program_id(0)

    @pl.when(i == 0)
    def _():
        carry_ref[...] = jnp.zeros_like(carry_ref)

    neg = -jnp.inf
    row8 = lax.broadcasted_iota(jnp.int32, (8, tm), 0)
    g = jnp.where(row8 < N_GROUPS, lg_ref[0:8, :], neg)
    gmax = jnp.max(g, axis=0, keepdims=True)
    gsum = jnp.sum(jnp.exp(g - gmax), axis=0, keepdims=True)
    p_grp = 1.0 / gsum
    grp = jnp.min(jnp.where(g == gmax, row8, 8), axis=0, keepdims=True)

    e8 = lg_ref[8:16, :]
    for gi in range(1, N_GROUPS):
        e8 = jnp.where(grp == gi, lg_ref[8 + 8 * gi:16 + 8 * gi, :], e8)
    m1 = jnp.max(e8, axis=0, keepdims=True)
    i1 = jnp.min(jnp.where(e8 == m1, row8, 8), axis=0, keepdims=True)
    e8b = jnp.where(row8 == i1, neg, e8)
    m2 = jnp.max(e8b, axis=0, keepdims=True)
    i2 = jnp.min(jnp.where(e8b == m2, row8, 8), axis=0, keepdims=True)
    zsum = jnp.sum(jnp.exp(e8 - m1), axis=0, keepdims=True)
    p1 = 1.0 / zsum
    p2 = jnp.exp(m2 - m1) / zsum
    den = p1 + p2
    w1 = p_grp * p1 / den
    w2 = p_grp * p2 / den
    eid1 = grp * EXPERTS_PER_GROUP + i1
    eid2 = grp * EXPERTS_PER_GROUP + i2

    rowe = lax.broadcasted_iota(jnp.int32, (N_EXPERTS, tm), 0)
    oh1 = (rowe == eid1)
    oh2 = (rowe == eid2)
    tr = lax.broadcasted_iota(jnp.int32, (tm, tm), 0)
    tc = lax.broadcasted_iota(jnp.int32, (tm, tm), 1)
    upper = (tr < tc).astype(BF16)
    cum1 = _dot(oh1.astype(BF16), upper)
    cum2 = _dot(oh2.astype(BF16), upper)
    oh1f = oh1.astype(F32)
    oh2f = oh2.astype(F32)
    tot1 = jnp.sum(oh1f, axis=1, keepdims=True)
    tot2 = jnp.sum(oh2f, axis=1, keepdims=True)
    carry = carry_ref[:, 0:1]
    rank1 = jnp.sum(oh1f * (cum1 + carry), axis=0, keepdims=True)
    rank2 = jnp.sum(oh2f * (cum2 + carry + tot1), axis=0, keepdims=True)
    new_carry = carry + tot1 + tot2
    carry_ref[...] = jnp.broadcast_to(new_carry, carry_ref.shape)
    cnt_ref[...] = jnp.broadcast_to(new_carry, cnt_ref.shape)

    zi = jnp.zeros((1, tm), jnp.int32)
    info_ref[...] = jnp.concatenate(
        [eid1, eid2, rank1.astype(jnp.int32), rank2.astype(jnp.int32), zi, zi, zi, zi], axis=0)
    zf = jnp.zeros((1, tm), F32)
    w_ref[...] = jnp.concatenate([w1, w2, zf, zf, zf, zf, zf, zf], axis=0)


def _route(logits_t, tm=512):
    t = logits_t.shape[1]
    return pl.pallas_call(
        functools.partial(_route_kernel, tm=tm),
        grid=(t // tm,),
        in_specs=[pl.BlockSpec((ROUTER_ROWS, tm), lambda i: (0, i))],
        out_specs=[
            pl.BlockSpec((8, tm), lambda i: (0, i)),
            pl.BlockSpec((8, tm), lambda i: (0, i)),
            pl.BlockSpec((N_EXPERTS, LANES), lambda i: (0, 0)),
        ],
        out_shape=[jax.ShapeDtypeStruct((8, t), jnp.int32), jax.ShapeDtypeStruct((8, t), F32),
                   jax.ShapeDtypeStruct((N_EXPERTS, LANES), F32)],
        scratch_shapes=[pltpu.VMEM((N_EXPERTS, LANES), F32)],
        compiler_params=_cparams(("arbitrary",)),
        name="route",
    )(logits_t)


MOE_BLK = 256


def _dispatch_kernel(zb_ref, dest_ref, h_ref, xs_ref, zero_ref, sem, zsem, *, tm, nb):
    blk_rows = MOE_BLK * ROW_TILE

    @pl.when(pl.program_id(0) == 0)
    def _():
        zero_ref[...] = jnp.zeros_like(zero_ref)

        def zero_copy(b):
            return pltpu.make_async_copy(zero_ref, xs_ref.at[pl.ds(b * blk_rows, blk_rows)], zsem)

        def zstart(b, carry):
            @pl.when(zb_ref[b] == 1)
            def _():
                zero_copy(b).start()
            return carry

        def zwait(b, carry):
            @pl.when(zb_ref[b] == 1)
            def _():
                zero_copy(b).wait()
            return carry

        lax.fori_loop(0, nb, zstart, 0)
        lax.fori_loop(0, nb, zwait, 0)

    def row_copy(t, d):
        src = h_ref.at[pl.ds(pl.multiple_of(t * ROW_TILE, ROW_TILE), ROW_TILE)]
        dst = xs_ref.at[pl.ds(pl.multiple_of(d * ROW_TILE, ROW_TILE), ROW_TILE)]
        return pltpu.make_async_copy(src, dst, sem)

    def issue(t, carry):
        row_copy(t, dest_ref[0, t]).start()
        row_copy(t, dest_ref[1, t]).start()
        return carry

    lax.fori_loop(0, tm, issue, 0, unroll=8)
    tile_wait = pltpu.make_async_copy(h_ref, xs_ref.at[pl.ds(0, tm * ROW_TILE)], sem)
    tile_wait.wait()
    tile_wait.wait()


def _dispatch(zero_blk, dest, h2p, nb, tm=512):
    t = h2p.shape[0] // ROW_TILE
    grid_spec = pltpu.PrefetchScalarGridSpec(
        num_scalar_prefetch=1,
        grid=(t // tm,),
        in_specs=[
            pl.BlockSpec((8, tm), lambda i, zb: (0, i), memory_space=pltpu.SMEM),
            pl.BlockSpec((tm * ROW_TILE, LANES), lambda i, zb: (i, 0)),
        ],
        out_specs=pl.BlockSpec(memory_space=pl.ANY),
        scratch_shapes=[pltpu.VMEM((MOE_BLK * ROW_TILE, LANES), jnp.uint32), pltpu.SemaphoreType.DMA(()),
                        pltpu.SemaphoreType.DMA(())],
    )
    return pl.pallas_call(
        functools.partial(_dispatch_kernel, tm=tm, nb=nb),
        grid_spec=grid_spec,
        out_shape=jax.ShapeDtypeStruct((nb * MOE_BLK * ROW_TILE, LANES), jnp.uint32),
        compiler_params=_cparams(("arbitrary",)),
        name="dispatch",
    )(zero_blk, dest, h2p)


def _expert_kernel(be_ref, nu_ref, x_ref, wg_ref, wu_ref, wd_ref, y_ref, wgb_ref, wub_ref, wdb_ref, *, d):
    b = pl.program_id(0)

    @pl.when(jnp.logical_or(b == 0, be_ref[b] != be_ref[jnp.maximum(b - 1, 0)]))
    def _():
        wgb_ref[...] = wg_ref[0].astype(BF16)
        wub_ref[...] = wu_ref[0].astype(BF16)
        wdb_ref[...] = wd_ref[0].astype(BF16)

    @pl.when(b < nu_ref[0])
    def _():
        lo, hi = _unpack_pair(_load_rows_tiled(x_ref, MOE_BLK))
        x = jnp.concatenate([lo, hi], axis=1).astype(BF16)
        gate = _dot(x, wgb_ref[...])
        up = _dot(x, wub_ref[...])
        hid = (gate * _sigmoid(gate) * up).astype(BF16)
        y = _dot(hid, wdb_ref[...])
        half = d // 2
        _store_rows_tiled(y_ref, _pack_pair(y[:, :half], y[:, half:]))

    @pl.when(b >= nu_ref[0])
    def _():
        y_ref[...] = jnp.zeros_like(y_ref)


def _experts(blk_expert, n_used, xs, wg, wu, wd, d):
    p = xs.shape[0] // ROW_TILE
    nb = p // MOE_BLK
    blk_rows = MOE_BLK * ROW_TILE

    def xmap(b, be, nu):
        return (jnp.minimum(b, nu[0] - 1), 0)

    grid_spec = pltpu.PrefetchScalarGridSpec(
        num_scalar_prefetch=2,
        grid=(nb,),
        in_specs=[
            pl.BlockSpec((blk_rows, LANES), xmap),
            pl.BlockSpec((1, d, D_EXPERT), lambda b, be, nu: (be[b], 0, 0)),
            pl.BlockSpec((1, d, D_EXPERT), lambda b, be, nu: (be[b], 0, 0)),
            pl.BlockSpec((1, D_EXPERT, d), lambda b, be, nu: (be[b], 0, 0)),
        ],
        out_specs=pl.BlockSpec((blk_rows, LANES), lambda b, be, nu: (b, 0)),
        scratch_shapes=[pltpu.VMEM((d, D_EXPERT), BF16), pltpu.VMEM((d, D_EXPERT), BF16),
                        pltpu.VMEM((D_EXPERT, d), BF16)],
    )
    return pl.pallas_call(
        functools.partial(_expert_kernel, d=d),
        grid_spec=grid_spec,
        out_shape=jax.ShapeDtypeStruct((p * ROW_TILE, LANES), jnp.uint32),
        compiler_params=_cparams(("arbitrary",)),
        name="experts",
    )(blk_expert, n_used, xs, wg, wu, wd)


COMBINE_GROUP = 8


def _combine_kernel(dcur_ref, dnxt_ref, x1_ref, wt_ref, yb_ref, op_ref, os_ref, buf_ref, sem, *, tm, npb, nsteps):
    i = pl.program_id(0)
    slot = lax.rem(i, 2)
    nslot = 1 - slot
    grp = COMBINE_GROUP

    def row_copy(s, k, t, d):
        src = yb_ref.at[pl.ds(pl.multiple_of(d * ROW_TILE, ROW_TILE), ROW_TILE)]
        dst = buf_ref.at[s, k, pl.ds(pl.multiple_of(t * ROW_TILE, ROW_TILE), ROW_TILE)]
        return pltpu.make_async_copy(src, dst, sem.at[s])

    def issue_group(dref, s, t0):
        for u in range(grp):
            row_copy(s, 0, t0 + u, dref[0, t0 + u]).start()
            row_copy(s, 1, t0 + u, dref[1, t0 + u]).start()

    def wait_slot(s):
        for k in range(TOP_K):
            pltpu.make_async_copy(yb_ref.at[pl.ds(0, tm * ROW_TILE)], buf_ref.at[s, k], sem.at[s]).wait()

    @pl.when(i == 0)
    def _():
        def first(g, carry):
            issue_group(dcur_ref, 0, g * grp)
            return carry

        lax.fori_loop(0, tm // grp, first, 0)

    wait_slot(slot)

    def run(o_ref):
        def group(g, carry):
            t0 = pl.multiple_of(g * grp, grp)
            issue_group(dnxt_ref, nslot, t0)
            rows = pl.ds(t0, grp)
            w = wt_ref[rows, :]
            w1 = w[:, 0:1]
            w2 = w[:, 1:2]
            a_lo, a_hi = _unpack_pair(_load_rows_tiled(buf_ref, grp, t0, lead=(slot, 0)))
            b_lo, b_hi = _unpack_pair(_load_rows_tiled(buf_ref, grp, t0, lead=(slot, 1)))
            moe = jnp.concatenate([a_lo * w1 + b_lo * w2, a_hi * w1 + b_hi * w2], axis=1)
            o_ref[rows, :] = x1_ref[rows, :] + moe
            return carry

        lax.fori_loop(0, tm // grp, group, 0)

    @pl.when(i < npb)
    def _():
        run(op_ref)

    @pl.when(i >= npb)
    def _():
        run(os_ref)

    @pl.when(i == nsteps - 1)
    def _():
        wait_slot(nslot)


def _combine(dest, x1, wt, yb, tp, tm=256):
    t, d = x1.shape
    npb = tp // tm
    nsteps = t // tm
    return pl.pallas_call(
        functools.partial(_combine_kernel, tm=tm, npb=npb, nsteps=nsteps),
        grid=(nsteps,),
        in_specs=[
            pl.BlockSpec((8, tm), lambda i: (0, i), memory_space=pltpu.SMEM),
            pl.BlockSpec((8, tm), lambda i: (0, jnp.minimum(i + 1, nsteps - 1)), memory_space=pltpu.SMEM),
            pl.BlockSpec((tm, d), lambda i: (i, 0)),
            pl.BlockSpec((tm, 8), lambda i: (i, 0)),
            pl.BlockSpec(memory_space=pl.ANY),
        ],
        out_specs=[
            pl.BlockSpec((tm, d), lambda i: (jnp.minimum(i, npb - 1), 0)),
            pl.BlockSpec((tm, d), lambda i: (jnp.maximum(i - npb, 0), 0)),
        ],
        out_shape=[jax.ShapeDtypeStruct((tp, d), F32), jax.ShapeDtypeStruct((t - tp, d), F32)],
        scratch_shapes=[pltpu.VMEM((2, TOP_K, tm * ROW_TILE, LANES), jnp.uint32), pltpu.SemaphoreType.DMA((2,))],
        compiler_params=_cparams(("arbitrary",)),
        name="combine",
    )(dest, dest, x1, wt, yb)


def _rope_tables(length):
    rows = length // GRID_W
    row = jnp.broadcast_to(jnp.arange(rows, dtype=F32)[:, None], (rows, GRID_W)).reshape(length)
    col = jnp.broadcast_to(jnp.arange(GRID_W, dtype=F32)[None, :], (rows, GRID_W)).reshape(length)
    axis_dim = HEAD_DIM // 2
    freqs = ROPE_THETA ** (-jnp.arange(0, axis_dim, 2, dtype=F32) / axis_dim)
    ang = jnp.concatenate([row[:, None] * freqs, col[:, None] * freqs], axis=-1)
    c, s = jnp.cos(ang), jnp.sin(ang)
    return jnp.concatenate([c, c], axis=1), jnp.concatenate([-s, s], axis=1)


def _layer(xp, xs, mem_all, seqs, n_mem, p):
    d = xp.shape[1]
    tp = xp.shape[0]
    lay = _proj_layout(d)

    w_in = p["w_in"]
    sizes = (GLA_KEY, GLA_KEY, GLA_VAL, GLA_VAL, GLA_RANK, GLA_RANK, GQA_Q, GQA_KV, GQA_KV, MEM_W, 3 * d)
    offs = np.concatenate([[0], np.cumsum(sizes)])
    seg = {n: w_in[:, offs[k]:offs[k + 1]] for k, n in enumerate(
        ("gq", "gk", "gv", "gg", "ga_f", "ga_b", "aq", "ak", "av", "mq", "mg"))}
    def deinterleave(w, heads):
        rows = w.shape[0]
        return w.reshape(rows, heads, HEAD_DIM // 2, 2).transpose(0, 1, 3, 2).reshape(rows, heads * HEAD_DIM)

    w_main = jnp.concatenate([seg["mg"], seg["gv"], seg["gg"], deinterleave(seg["aq"], GQA_HEADS), seg["mq"],
                              seg["gq"], seg["gk"], deinterleave(seg["ak"], GQA_KV_HEADS), seg["av"]],
                             axis=1).astype(BF16)
    w_ga = jnp.concatenate([seg["ga_f"], seg["ga_b"], jnp.zeros((d, LANES - 2 * GLA_RANK), F32)],
                           axis=1).astype(BF16)

    def a2_pad(a2, row0):
        pad = [jnp.zeros((row0, GLA_KEY), F32)] if row0 else []
        pad += [a2, jnp.zeros((LANES - row0 - GLA_RANK, GLA_KEY), F32)]
        return jnp.concatenate(pad, axis=0).astype(BF16)

    gq_perm = deinterleave(p["g_q_gqa"][None, :], 1)
    gk_perm = deinterleave(p["g_k_gqa"][None, :], 1)

    proj, ga = _inproj(xp, xs, p["g_mix"][None, :], w_main, w_ga)

    o_f = _gla(proj, ga, a2_pad(p["gla_a2_fwd"], 0), p["gla_ab_fwd"][None, :], lay, seqs, False)
    o_b = _gla(proj, ga, a2_pad(p["gla_a2_bwd"], GLA_RANK), p["gla_ab_bwd"][None, :], lay, seqs, True)

    max_len = max(length for _, length in seqs)
    cos_t, sin_t = _rope_tables(max_len)
    qt3, ka, vt3 = _qkprep(proj, cos_t, sin_t, gq_perm, gk_perm, lay, seqs)
    groups = []
    for (start, length) in seqs:
        if groups and groups[-1][2] == length and groups[-1][0] + groups[-1][1] * length == start:
            groups[-1][1] += 1
        else:
            groups.append([start, 1, length])
    o_gqa = jnp.zeros((proj.shape[0], GQA_Q), BF16)
    for (start, nseq, length) in groups:
        o_gqa = _flash_group(qt3, ka, vt3, start, nseq, length, o_gqa)

    km, vm = _memkv(mem_all, p["g_mem_norm"][None, :], p["w_mem_kv"].astype(BF16), p["g_k_mem"][None, :])
    o_mem = _memattn(proj, km, vm, p["g_q_mem"][None, :], lay, seqs, n_mem)

    merged = _merge(o_f, o_b, proj, o_gqa, o_mem, p["g_gla_out"][None, :], p["w_br_gla"].astype(BF16),
                    p["w_br_gqa"].astype(BF16), p["w_br_mem"].astype(BF16), lay, d)
    gpad = 8 - N_GROUPS
    wr = jnp.concatenate([p["w_router_group"].T, jnp.zeros((gpad, d), F32), p["w_router_expert"].T], axis=0)
    wr_hi = wr.astype(BF16)
    wr_lo = (wr - wr_hi.astype(F32)).astype(BF16)
    br = jnp.concatenate([p["b_router_group"], jnp.zeros((gpad,), F32), p["b_router_expert"]])
    br = jnp.broadcast_to(br[:, None], (ROUTER_ROWS, LANES))
    x1, h2p, logits_t = _outproj(merged, xp, xs, p["w_out"].astype(BF16), p["g_ffn"][None, :], wr_hi, wr_lo, br)

    info, wts, counts = _route(logits_t)
    t = x1.shape[0]
    cnt = counts[:, 0].astype(jnp.int32)
    padded = ((cnt + MOE_BLK - 1) // MOE_BLK) * MOE_BLK
    pad_ends = jnp.cumsum(padded)
    pad_starts = pad_ends - padded
    nb = (t * TOP_K) // MOE_BLK + N_EXPERTS
    n_used = (pad_ends[-1] // MOE_BLK).astype(jnp.int32)
    blk = jnp.arange(nb, dtype=jnp.int32)
    first_row = jnp.minimum(blk, n_used - 1) * MOE_BLK
    blk_expert = jnp.sum((pad_ends[None, :] <= first_row[:, None]).astype(jnp.int32), axis=1)
    blk_expert = jnp.minimum(blk_expert, N_EXPERTS - 1)
    ragged = jnp.logical_and(cnt > 0, cnt % MOE_BLK != 0)
    is_last = jnp.any(jnp.logical_and((pad_ends // MOE_BLK - 1)[None, :] == blk[:, None], ragged[None, :]), axis=1)
    zero_blk = jnp.logical_or(is_last, blk >= n_used).astype(jnp.int32)
    eid = info[0:2]
    sel = eid[None] == jnp.arange(N_EXPERTS, dtype=jnp.int32)[:, None, None]
    dest = info[2:4] + jnp.sum(jnp.where(sel, pad_starts[:, None, None], 0), axis=0)
    dest = jnp.concatenate([dest, jnp.zeros((6, t), jnp.int32)], axis=0)

    xs_sorted = _dispatch(zero_blk, dest, h2p, nb)
    yb = _experts(blk_expert, n_used[None], xs_sorted, p["w_exp_gate"], p["w_exp_up"], p["w_exp_down"], d)
    return _combine(dest, x1, wts.T, yb, tp)


def kernel(x_prompt, x_sample, mem_prompt, mem_sample, g_mix, w_in, gla_a2_fwd, gla_ab_fwd, gla_a2_bwd, gla_ab_bwd, g_gla_out, g_q_gqa, g_k_gqa, g_mem_norm, w_mem_kv, g_q_mem, g_k_mem, w_br_gla, w_br_gqa, w_br_mem, w_out, g_ffn, w_router_group, b_router_group, w_router_expert, b_router_expert, w_exp_gate, w_exp_up, w_exp_down):
    bp, lp, d = x_prompt.shape
    bs, ls, _ = x_sample.shape
    n_mem = mem_prompt.shape[1]
    depth = g_mix.shape[0]
    seqs = [(b * lp, lp) for b in range(bp)] + [(bp * lp + b * ls, ls) for b in range(bs)]
    xp = x_prompt.reshape(bp * lp, d)
    xs = x_sample.reshape(bs * ls, d)
    mem_all = jnp.concatenate([mem_prompt.reshape(bp * n_mem, d), mem_sample.reshape(bs * n_mem, d)], axis=0)
    names = ("g_mix", "w_in", "gla_a2_fwd", "gla_ab_fwd", "gla_a2_bwd", "gla_ab_bwd", "g_gla_out", "g_q_gqa",
             "g_k_gqa", "g_mem_norm", "w_mem_kv", "g_q_mem", "g_k_mem", "w_br_gla", "w_br_gqa", "w_br_mem",
             "w_out", "g_ffn", "w_router_group", "b_router_group", "w_router_expert", "b_router_expert",
             "w_exp_gate", "w_exp_up", "w_exp_down")
    vals = (g_mix, w_in, gla_a2_fwd, gla_ab_fwd, gla_a2_bwd, gla_ab_bwd, g_gla_out, g_q_gqa, g_k_gqa,
            g_mem_norm, w_mem_kv, g_q_mem, g_k_mem, w_br_gla, w_br_gqa, w_br_mem, w_out, g_ffn,
            w_router_group, b_router_group, w_router_expert, b_router_expert, w_exp_gate, w_exp_up, w_exp_down)
    for layer in range(depth):
        p = {n: v[layer] for n, v in zip(names, vals)}
        xp, xs = _layer(xp, xs, mem_all, seqs, n_mem, p)
    return xp.reshape(bp, lp, d), xs.reshape(bs, ls, d)
```

```python
import functools

import numpy as np
import jax
import jax.numpy as jnp
from jax import lax
from jax.experimental import pallas as pl
from jax.experimental.pallas import tpu as pltpu

F32 = jnp.float32
BF16 = jnp.bfloat16

GRID_W = 64
GLA_HEADS, GLA_DK, GLA_DV = 4, 128, 256
GLA_KEY, GLA_VAL = GLA_HEADS * GLA_DK, GLA_HEADS * GLA_DV
GLA_RANK, GLA_TAU, GLA_CHUNK = 16, 16.0, 64
GQA_HEADS, GQA_KV_HEADS, HEAD_DIM = 8, 2, 128
GQA_GROUP = GQA_HEADS // GQA_KV_HEADS
GQA_Q, GQA_KV = GQA_HEADS * HEAD_DIM, GQA_KV_HEADS * HEAD_DIM
ROPE_THETA = 10000.0
MEM_HEADS, MEM_HD = 4, 256
MEM_W = MEM_HEADS * MEM_HD
N_GROUPS, EXPERTS_PER_GROUP, TOP_K, D_EXPERT = 4, 8, 2, 512
N_EXPERTS = N_GROUPS * EXPERTS_PER_GROUP
EPS = 1e-6

LANES = 128
VMEM_LIMIT_BYTES = 56 * 1024 * 1024


def _proj_layout(d):
    off = {}
    c = 0
    for name, w in (("mg", 3 * d), ("gv", GLA_VAL), ("gg", GLA_VAL), ("aq", GQA_Q), ("mq", MEM_W),
                    ("gq", GLA_KEY), ("gk", GLA_KEY), ("ak", GQA_KV), ("av", GQA_KV)):
        off[name] = c
        c += w
    off["total"] = c
    return off


def _cparams(sem, vmem=VMEM_LIMIT_BYTES):
    return pltpu.CompilerParams(dimension_semantics=sem, vmem_limit_bytes=vmem)


def _rms(x, g):
    ms = jnp.mean(x * x, axis=-1, keepdims=True)
    return x * lax.rsqrt(ms + EPS) * g


def _dot(a, b):
    return jnp.dot(a, b, preferred_element_type=F32)


def _dot_nt(a, b):
    return lax.dot_general(a, b, (((1,), (1,)), ((), ())), preferred_element_type=F32)


def _dot_tn(a, b):
    return lax.dot_general(a, b, (((0,), (0,)), ((), ())), preferred_element_type=F32)


def _sigmoid(x):
    return 1.0 / (1.0 + jnp.exp(-x))


def _pack_pair(lo, hi):
    lo_b = pltpu.bitcast(lo.astype(BF16).astype(F32), jnp.uint32)
    hi_b = pltpu.bitcast(hi.astype(BF16).astype(F32), jnp.uint32)
    return (hi_b & jnp.uint32(0xFFFF0000)) | (lo_b >> 16)


def _unpack_pair(w):
    lo = pltpu.bitcast(w << 16, F32)
    hi = pltpu.bitcast(w & jnp.uint32(0xFFFF0000), F32)
    return lo, hi


ROW_TILE = 8


def _store_rows_tiled(ref, packed, start=0):
    n = packed.shape[0]
    for c in range(ROW_TILE):
        ref[pl.ds(start * ROW_TILE + c, n, stride=ROW_TILE), :] = packed[:, c * LANES:(c + 1) * LANES]


def _load_rows_tiled(ref, n, start=0, lead=()):
    return jnp.concatenate(
        [ref[lead + (pl.ds(start * ROW_TILE + c, n, stride=ROW_TILE), slice(None))] for c in range(ROW_TILE)],
        axis=1)


def _inproj_kernel(xp_ref, xs_ref, g_ref, w_ref, wga_ref, o_ref, ga_ref, h_ref, *, npb):
    i = pl.program_id(0)
    j = pl.program_id(1)

    def norm(x_ref):
        hb = _rms(x_ref[...], g_ref[...]).astype(BF16)
        h_ref[...] = hb
        ga_ref[...] = _dot(hb, wga_ref[...])

    @pl.when(jnp.logical_and(j == 0, i < npb))
    def _():
        norm(xp_ref)

    @pl.when(jnp.logical_and(j == 0, i >= npb))
    def _():
        norm(xs_ref)

    o_ref[...] = _dot(h_ref[...], w_ref[...]).astype(BF16)


def _inproj(xp, xs, g, w_main, w_ga, tm=1024, tn=512):
    tp, d = xp.shape
    ts = xs.shape[0]
    t = tp + ts
    nc = w_main.shape[1]
    npb = tp // tm
    grid = (t // tm, nc // tn)
    return pl.pallas_call(
        functools.partial(_inproj_kernel, npb=npb),
        grid=grid,
        in_specs=[
            pl.BlockSpec((tm, d), lambda i, j: (jnp.minimum(i, npb - 1), 0)),
            pl.BlockSpec((tm, d), lambda i, j: (jnp.maximum(i - npb, 0), 0)),
            pl.BlockSpec((1, d), lambda i, j: (0, 0)),
            pl.BlockSpec((d, tn), lambda i, j: (0, j)),
            pl.BlockSpec((d, LANES), lambda i, j: (0, 0)),
        ],
        out_specs=[
            pl.BlockSpec((tm, tn), lambda i, j: (i, j)),
            pl.BlockSpec((tm, LANES), lambda i, j: (i, 0)),
        ],
        out_shape=[jax.ShapeDtypeStruct((t, nc), BF16), jax.ShapeDtypeStruct((t, LANES), F32)],
        scratch_shapes=[pltpu.VMEM((tm, d), BF16)],
        compiler_params=_cparams(("arbitrary", "arbitrary")),
        name="inproj",
    )(xp, xs, g, w_main, w_ga)


GLA_TB = 512


def _gla_kernel(rb_ref, first_ref, q_ref, k_ref, v_ref, ga_ref, a2_ref, ab_ref, o_ref, s_ref, *, reverse):
    j = pl.program_id(0)
    tb, c = GLA_TB, GLA_CHUNK
    nch = tb // c

    @pl.when(first_ref[j] == 1)
    def _():
        s_ref[...] = jnp.zeros_like(s_ref)

    r = lax.broadcasted_iota(jnp.int32, (tb, tb), 0)
    cc = lax.broadcasted_iota(jnp.int32, (tb, tb), 1)
    same = (r >> 6) == (cc >> 6)
    if reverse:
        tri = (cc >= r).astype(BF16)
        amask = jnp.logical_and(same, cc > r)
    else:
        tri = (cc <= r).astype(BF16)
        amask = jnp.logical_and(same, cc <= r)

    z = _dot(ga_ref[...].astype(BF16), a2_ref[...]) + ab_ref[...]
    la = (jnp.minimum(z, 0.0) - jnp.log1p(jnp.exp(-jnp.abs(z)))) * (1.0 / GLA_TAU)
    la_hi = la.astype(BF16)
    la_lo = (la - la_hi.astype(F32)).astype(BF16)
    pre_all = _dot(tri, la_hi) + _dot(tri, la_lo)
    scale = GLA_DK ** -0.5
    for h in range(GLA_HEADS):
        pre = pre_all[:, h * GLA_DK:(h + 1) * GLA_DK]
        bs, tots = [], []
        for ci in range(nch):
            r0 = ci * c
            if reverse:
                base = pre[r0 + c:r0 + c + 1] if ci < nch - 1 else jnp.zeros((1, GLA_DK), F32)
                tot = pre[r0:r0 + 1] - base
            else:
                base = pre[r0 - 1:r0] if ci > 0 else jnp.zeros((1, GLA_DK), F32)
                tot = pre[r0 + c - 1:r0 + c] - base
            bs.append(pre[r0:r0 + c] - base)
            tots.append(tot)
        b = jnp.concatenate(bs, axis=0)
        btot = jnp.concatenate([jnp.broadcast_to(tt, (c, GLA_DK)) for tt in tots], axis=0)

        q = q_ref[:, h * GLA_DK:(h + 1) * GLA_DK].astype(F32) * scale
        k = k_ref[:, h * GLA_DK:(h + 1) * GLA_DK].astype(F32)
        v = v_ref[:, h * GLA_DV:(h + 1) * GLA_DV]
        q_e = (q * jnp.exp(b)).astype(BF16)
        k_e = (k * jnp.exp(-b)).astype(BF16)
        k_d = (k * jnp.exp(btot - b)).astype(BF16)
        att = jnp.where(amask, _dot_nt(q_e, k_e), 0.0).astype(BF16)
        o_intra = _dot(att, v)

        st = s_ref[h]
        outs = [None] * nch
        order = range(nch - 1, -1, -1) if reverse else range(nch)
        for ci in order:
            r0 = ci * c
            o_inter = _dot_nt(q_e[r0:r0 + c], st.astype(BF16))
            outs[ci] = o_intra[r0:r0 + c] + o_inter
            kvt = _dot_tn(v[r0:r0 + c], k_d[r0:r0 + c])
            st = st * jnp.exp(tots[ci]) + kvt
        s_ref[h] = st
        o_ref[:, h * GLA_DV:(h + 1) * GLA_DV] = jnp.concatenate(outs, axis=0).astype(BF16)


def _gla(proj, ga, a2p, ab, lay, seqs, reverse):
    t = proj.shape[0]
    tb = GLA_TB
    rb, first = [], []
    for (start, length) in seqs:
        blocks = list(range(start // tb, (start + length) // tb))
        if reverse:
            blocks = blocks[::-1]
        rb += blocks
        first += [1] + [0] * (len(blocks) - 1)
    rb = jnp.asarray(np.array(rb, np.int32))
    first = jnp.asarray(np.array(first, np.int32))
    qb, kb, vb = lay["gq"] // GLA_KEY, lay["gk"] // GLA_KEY, lay["gv"] // GLA_VAL
    grid_spec = pltpu.PrefetchScalarGridSpec(
        num_scalar_prefetch=2,
        grid=(t // tb,),
        in_specs=[
            pl.BlockSpec((tb, GLA_KEY), lambda j, rbr, fr: (rbr[j], qb)),
            pl.BlockSpec((tb, GLA_KEY), lambda j, rbr, fr: (rbr[j], kb)),
            pl.BlockSpec((tb, GLA_VAL), lambda j, rbr, fr: (rbr[j], vb)),
            pl.BlockSpec((tb, LANES), lambda j, rbr, fr: (rbr[j], 0)),
            pl.BlockSpec((LANES, GLA_KEY), lambda j, rbr, fr: (0, 0)),
            pl.BlockSpec((1, GLA_KEY), lambda j, rbr, fr: (0, 0)),
        ],
        out_specs=pl.BlockSpec((tb, GLA_VAL), lambda j, rbr, fr: (rbr[j], 0)),
        scratch_shapes=[pltpu.VMEM((GLA_HEADS, GLA_DV, GLA_DK), F32)],
    )
    return pl.pallas_call(
        functools.partial(_gla_kernel, reverse=reverse),
        grid_spec=grid_spec,
        out_shape=jax.ShapeDtypeStruct((t, GLA_VAL), BF16),
        compiler_params=_cparams(("arbitrary",)),
        name="gla_bwd" if reverse else "gla_fwd",
    )(rb, first, proj, proj, proj, ga, a2p, ab)


FLASH_TQ = 256
FLASH_TK = 512
LOG2E = 1.4426950408889634
BF16_SUBLANES = 16
VT_ROWS = HEAD_DIM + BF16_SUBLANES


def _qkprep_kernel(pb_ref, aq_ref, ak_ref, av_ref, cos_ref, sin_ref, gq_ref, gk_ref, qt_ref, ka_ref, vt_ref):
    cosv = cos_ref[...]
    sinv = sin_ref[...]
    scale = HEAD_DIM ** -0.5 * LOG2E
    nsub = FLASH_TK // FLASH_TQ

    def one(x, g):
        xn = _rms(x.astype(F32), g)
        return xn * cosv + pltpu.roll(xn, HEAD_DIM // 2, 1) * sinv

    for h in range(GQA_HEADS):
        sl = slice(h * HEAD_DIM, (h + 1) * HEAD_DIM)
        qt = (one(aq_ref[:, sl], gq_ref[...]) * scale).T.astype(BF16)
        for u in range(nsub):
            qt_ref[u, sl, :] = qt[:, u * FLASH_TQ:(u + 1) * FLASH_TQ]
    for h in range(GQA_KV_HEADS):
        sl = slice(h * HEAD_DIM, (h + 1) * HEAD_DIM)
        ka_ref[:, sl] = one(ak_ref[:, sl], gk_ref[...]).astype(BF16)
        vt_ref[0, h * VT_ROWS:h * VT_ROWS + HEAD_DIM, :] = av_ref[:, sl].astype(F32).T.astype(BF16)
        vt_ref[0, h * VT_ROWS + HEAD_DIM:(h + 1) * VT_ROWS, :] = jnp.ones((BF16_SUBLANES, FLASH_TK), BF16)


def _qkprep(proj, cos_t, sin_t, gq, gk, lay, seqs):
    t = proj.shape[0]
    tm = FLASH_TK
    nsub = FLASH_TK // FLASH_TQ
    pb = []
    for (start, length) in seqs:
        pb += list(range(length // tm))
    pb = jnp.asarray(np.array(pb, np.int32))
    aqb, akb = lay["aq"] // GQA_Q, lay["ak"] // GQA_KV
    grid_spec = pltpu.PrefetchScalarGridSpec(
        num_scalar_prefetch=1,
        grid=(t // tm,),
        in_specs=[
            pl.BlockSpec((tm, GQA_Q), lambda i, p: (i, aqb)),
            pl.BlockSpec((tm, GQA_KV), lambda i, p: (i, akb)),
            pl.BlockSpec((tm, GQA_KV), lambda i, p: (i, akb + 1)),
            pl.BlockSpec((tm, HEAD_DIM), lambda i, p: (p[i], 0)),
            pl.BlockSpec((tm, HEAD_DIM), lambda i, p: (p[i], 0)),
            pl.BlockSpec((1, HEAD_DIM), lambda i, p: (0, 0)),
            pl.BlockSpec((1, HEAD_DIM), lambda i, p: (0, 0)),
        ],
        out_specs=[
            pl.BlockSpec((nsub, GQA_Q, FLASH_TQ), lambda i, p: (i, 0, 0)),
            pl.BlockSpec((tm, GQA_KV), lambda i, p: (i, 0)),
            pl.BlockSpec((1, GQA_KV_HEADS * VT_ROWS, tm), lambda i, p: (i, 0, 0)),
        ],
    )
    return pl.pallas_call(
        _qkprep_kernel,
        grid_spec=grid_spec,
        out_shape=[jax.ShapeDtypeStruct((t // FLASH_TQ, GQA_Q, FLASH_TQ), BF16),
                   jax.ShapeDtypeStruct((t, GQA_KV), BF16),
                   jax.ShapeDtypeStruct((t // tm, GQA_KV_HEADS * VT_ROWS, tm), BF16)],
        compiler_params=_cparams(("arbitrary",)),
        name="qkprep",
    )(pb, proj, proj, proj, cos_t, sin_t, gq, gk)


FLASH_REFRAME = 64.0


def _flash_kernel(q_ref, k_ref, v_ref, prev_ref, o_ref, acc_ref, p0_ref, p1_ref, frame_ref, mcur_ref, *, nk):
    del prev_ref
    tq, tk = FLASH_TQ, FLASH_TK
    qt = jnp.concatenate([q_ref[0, g * HEAD_DIM:(g + 1) * HEAD_DIM, :] for g in range(GQA_GROUP)], axis=1)

    def scores(ci):
        off = pl.multiple_of(ci * tk, tk)
        return _dot(k_ref[pl.ds(off, tk), :], qt)

    def reframe(ci):
        c_new = jnp.maximum(frame_ref[...], mcur_ref[...])
        alpha = jnp.exp2(frame_ref[...] - c_new)
        pt = jnp.exp2(scores(ci) - c_new).astype(BF16)
        acc_ref[...] = alpha * acc_ref[...] + _dot(v_ref[ci], pt)
        frame_ref[...] = c_new

    st0 = scores(0)
    frame_ref[...] = jnp.max(st0, axis=0, keepdims=True)
    acc_ref[...] = _dot(v_ref[0], jnp.exp2(st0 - frame_ref[...]).astype(BF16))
    p_refs = (p0_ref, p1_ref)
    p0_ref[...] = jnp.zeros(p0_ref.shape, BF16)

    def step(ci, par, need_prev):
        cur_ref, prv_ref = p_refs[par], p_refs[1 - par]

        @pl.when(need_prev)
        def _():
            reframe(ci - 1)
            prv_ref[...] = jnp.zeros(prv_ref.shape, BF16)

        st = scores(ci)
        m_cur = jnp.max(st, axis=0, keepdims=True)
        mcur_ref[...] = m_cur
        cur_ref[...] = jnp.exp2(st - frame_ref[...]).astype(BF16)
        acc_ref[...] += _dot(v_ref[ci - 1], prv_ref[...])
        return jnp.max(m_cur - frame_ref[...]) > FLASH_REFRAME

    def pair(j, need_prev):
        ci = 1 + 2 * j
        return step(ci + 1, 0, step(ci, 1, need_prev))

    need_last = lax.fori_loop(0, (nk - 1) // 2, pair, False)
    if (nk - 1) % 2:
        need_last = step(nk - 1, (nk - 1) % 2, need_last)
    last = nk - 1

    @pl.when(need_last)
    def _():
        reframe(last)

    @pl.when(jnp.logical_not(need_last))
    def _():
        acc_ref[...] += _dot(v_ref[last], p_refs[last % 2][...])

    out_t = acc_ref[0:HEAD_DIM, :] / acc_ref[HEAD_DIM:HEAD_DIM + 1, :]
    for g in range(GQA_GROUP):
        o_ref[:, g * HEAD_DIM:(g + 1) * HEAD_DIM] = out_t[:, g * tq:(g + 1) * tq].T.astype(BF16)


def _flash_group(qt3, ka, vt3, start, nseq, length, prev):
    t = ka.shape[0]
    tq, tk = FLASH_TQ, FLASH_TK
    nqt = length // tq
    nk = length // tk
    qb0 = start // tq
    sb0 = start // length
    gw = GQA_GROUP * HEAD_DIM
    in_specs = [
        pl.BlockSpec((1, gw, tq), lambda s, kv, i: (qb0 + s * nqt + i, kv, 0)),
        pl.BlockSpec((length, HEAD_DIM), lambda s, kv, i: (sb0 + s, kv)),
        pl.BlockSpec((nk, VT_ROWS, tk), lambda s, kv, i: (sb0 + s, kv, 0)),
        pl.BlockSpec(memory_space=pl.ANY),
    ]
    return pl.pallas_call(
        functools.partial(_flash_kernel, nk=nk),
        grid=(nseq, GQA_KV_HEADS, nqt),
        in_specs=in_specs,
        out_specs=pl.BlockSpec((tq, gw), lambda s, kv, i: (qb0 + s * nqt + i, kv)),
        out_shape=jax.ShapeDtypeStruct((t, GQA_Q), BF16),
        scratch_shapes=[pltpu.VMEM((VT_ROWS, GQA_GROUP * tq), F32),
                        pltpu.VMEM((tk, GQA_GROUP * tq), BF16),
                        pltpu.VMEM((tk, GQA_GROUP * tq), BF16),
                        pltpu.VMEM((1, GQA_GROUP * tq), F32),
                        pltpu.VMEM((1, GQA_GROUP * tq), F32)],
        input_output_aliases={3: 0},
        compiler_params=_cparams(("arbitrary", "arbitrary", "arbitrary")),
        name="gqa_flash",
    )(qt3, ka, vt3, prev)


def _memkv_kernel(mem_ref, g_ref, w_ref, gk_ref, km_ref, vm_ref):
    hb = _rms(mem_ref[...], g_ref[...]).astype(BF16)
    kv = _dot(hb, w_ref[...])
    for h in range(MEM_HEADS):
        sl = slice(h * MEM_HD, (h + 1) * MEM_HD)
        km_ref[:, sl] = _rms(kv[:, sl], gk_ref[...]).astype(BF16)
    vm_ref[...] = kv[:, MEM_W:].astype(BF16)


def _memkv(mem, g, w, gk):
    rows, d = mem.shape
    m = 256
    return pl.pallas_call(
        _memkv_kernel,
        grid=(rows // m,),
        in_specs=[
            pl.BlockSpec((m, d), lambda i: (i, 0)),
            pl.BlockSpec((1, d), lambda i: (0, 0)),
            pl.BlockSpec((d, 2 * MEM_W), lambda i: (0, 0)),
            pl.BlockSpec((1, MEM_HD), lambda i: (0, 0)),
        ],
        out_specs=[pl.BlockSpec((m, MEM_W), lambda i: (i, 0)), pl.BlockSpec((m, MEM_W), lambda i: (i, 0))],
        out_shape=[jax.ShapeDtypeStruct((rows, MEM_W), BF16), jax.ShapeDtypeStruct((rows, MEM_W), BF16)],
        compiler_params=_cparams(("arbitrary",)),
        name="memkv",
    )(mem, g, w, gk)


def _memattn_kernel(sq_ref, mq_ref, km_ref, vm_ref, gq_ref, o_ref):
    scale = MEM_HD ** -0.5
    for h in range(MEM_HEADS):
        sl = slice(h * MEM_HD, (h + 1) * MEM_HD)
        qn = (_rms(mq_ref[:, sl].astype(F32), gq_ref[...]) * scale).astype(BF16)
        s = _dot_nt(qn, km_ref[:, sl])
        m = jnp.max(s, axis=1, keepdims=True)
        p = jnp.exp(s - m)
        l = jnp.sum(p, axis=1, keepdims=True)
        o_ref[:, sl] = (_dot(p.astype(BF16), vm_ref[:, sl]) / l).astype(BF16)


def _memattn(proj, km, vm, gq, lay, seqs, n_mem, tm=512):
    t = proj.shape[0]
    sq = []
    for si, (start, length) in enumerate(seqs):
        sq += [si] * (length // tm)
    sq = jnp.asarray(np.array(sq, np.int32))
    mqb = lay["mq"] // MEM_W
    grid_spec = pltpu.PrefetchScalarGridSpec(
        num_scalar_prefetch=1,
        grid=(t // tm,),
        in_specs=[
            pl.BlockSpec((tm, MEM_W), lambda i, s: (i, mqb)),
            pl.BlockSpec((n_mem, MEM_W), lambda i, s: (s[i], 0)),
            pl.BlockSpec((n_mem, MEM_W), lambda i, s: (s[i], 0)),
            pl.BlockSpec((1, MEM_HD), lambda i, s: (0, 0)),
        ],
        out_specs=pl.BlockSpec((tm, MEM_W), lambda i, s: (i, 0)),
    )
    return pl.pallas_call(
        _memattn_kernel,
        grid_spec=grid_spec,
        out_shape=jax.ShapeDtypeStruct((t, MEM_W), BF16),
        compiler_params=_cparams(("arbitrary",)),
        name="memattn",
    )(sq, proj, km, vm, gq)


def _merge_kernel(of_ref, ob_ref, gg_ref, og_ref, om_ref, mg_ref, ggla_ref, w0_ref, w1_ref, w2_ref, o_ref, *, d):
    o = of_ref[...].astype(F32) + ob_ref[...].astype(F32)
    gg = gg_ref[...].astype(F32)
    parts = []
    for h in range(GLA_HEADS):
        sl = slice(h * GLA_DV, (h + 1) * GLA_DV)
        gh = gg[:, sl]
        parts.append((_rms(o[:, sl], ggla_ref[...]) * (gh * _sigmoid(gh))).astype(BF16))
    y0 = jnp.concatenate(parts, axis=1)
    acc = _sigmoid(mg_ref[:, 0:d].astype(F32)) * _dot(y0, w0_ref[...])
    acc = acc + _sigmoid(mg_ref[:, d:2 * d].astype(F32)) * _dot(og_ref[...], w1_ref[...])
    acc = acc + _sigmoid(mg_ref[:, 2 * d:3 * d].astype(F32)) * _dot(om_ref[...], w2_ref[...])
    o_ref[...] = acc.astype(BF16)


def _const_spec(shape):
    nd = len(shape)
    return pl.BlockSpec(shape, lambda i: (0,) * nd, pipeline_mode=pl.Buffered(1))


def _merge(o_f, o_b, proj, o_gqa, o_mem, g_gla, w0, w1, w2, lay, d, tm=512):
    t = proj.shape[0]
    ggb = lay["gg"] // GLA_VAL
    return pl.pallas_call(
        functools.partial(_merge_kernel, d=d),
        grid=(t // tm,),
        in_specs=[
            pl.BlockSpec((tm, GLA_VAL), lambda i: (i, 0)),
            pl.BlockSpec((tm, GLA_VAL), lambda i: (i, 0)),
            pl.BlockSpec((tm, GLA_VAL), lambda i: (i, ggb)),
            pl.BlockSpec((tm, GQA_Q), lambda i: (i, 0)),
            pl.BlockSpec((tm, MEM_W), lambda i: (i, 0)),
            pl.BlockSpec((tm, 3 * d), lambda i: (i, 0)),
            _const_spec((1, GLA_DV)),
            _const_spec((GLA_VAL, d)),
            _const_spec((GQA_Q, d)),
            _const_spec((MEM_W, d)),
        ],
        out_specs=pl.BlockSpec((tm, d), lambda i: (i, 0)),
        out_shape=jax.ShapeDtypeStruct((t, d), BF16),
        compiler_params=_cparams(("arbitrary",)),
        name="merge",
    )(o_f, o_b, proj, o_gqa, o_mem, proj, g_gla, w0, w1, w2)


ROUTER_ROWS = 40


def _outproj_kernel(mrg_ref, xp_ref, xs_ref, wo_ref, gf_ref, wrh_ref, wrl_ref, br_ref,
                    x1_ref, h2p_ref, lg_ref, *, npb, d):
    i = pl.program_id(0)
    x = jnp.where(i < npb, xp_ref[...], xs_ref[...])
    x1 = x + _dot(mrg_ref[...], wo_ref[...])
    x1_ref[...] = x1
    h2 = _rms(x1, gf_ref[...])
    h_hi = h2.astype(BF16)
    h_lo = (h2 - h_hi.astype(F32)).astype(BF16)
    lg = _dot_nt(wrh_ref[...], h_hi) + _dot_nt(wrh_ref[...], h_lo) + _dot_nt(wrl_ref[...], h_hi)
    lg_ref[...] = lg + br_ref[:, 0:1]
    half = d // 2
    _store_rows_tiled(h2p_ref, _pack_pair(h2[:, :half], h2[:, half:]))


def _outproj(merged, xp, xs, wo, gf, wr_hi, wr_lo, br, tm=512):
    tp, d = xp.shape
    t = merged.shape[0]
    npb = tp // tm
    assert d // 2 == ROW_TILE * LANES, "packed token rows must fill exactly one (8, 128) tile"
    return pl.pallas_call(
        functools.partial(_outproj_kernel, npb=npb, d=d),
        grid=(t // tm,),
        in_specs=[
            pl.BlockSpec((tm, d), lambda i: (i, 0)),
            pl.BlockSpec((tm, d), lambda i: (jnp.minimum(i, npb - 1), 0)),
            pl.BlockSpec((tm, d), lambda i: (jnp.maximum(i - npb, 0), 0)),
            _const_spec((d, d)),
            _const_spec((1, d)),
            _const_spec((ROUTER_ROWS, d)),
            _const_spec((ROUTER_ROWS, d)),
            _const_spec((ROUTER_ROWS, LANES)),
        ],
        out_specs=[
            pl.BlockSpec((tm, d), lambda i: (i, 0)),
            pl.BlockSpec((tm * ROW_TILE, LANES), lambda i: (i, 0)),
            pl.BlockSpec((ROUTER_ROWS, tm), lambda i: (0, i)),
        ],
        out_shape=[jax.ShapeDtypeStruct((t, d), F32), jax.ShapeDtypeStruct((t * ROW_TILE, LANES), jnp.uint32),
                   jax.ShapeDtypeStruct((ROUTER_ROWS, t), F32)],
        compiler_params=_cparams(("arbitrary",)),
        name="outproj",
    )(merged, xp, xs, wo, gf, wr_hi, wr_lo, br)


def _route_kernel(lg_ref, info_ref, w_ref, cnt_ref, carry_ref, *, tm):
    i = pl.program_id(0)

    @pl.when(i == 0)
    def _():
        carry_ref[...] = jnp.zeros_like(carry_ref)

    neg = -jnp.inf
    row8 = lax.broadcasted_iota(jnp.int32, (8, tm), 0)
    g = jnp.where(row8 < N_GROUPS, lg_ref[0:8, :], neg)
    gmax = jnp.max(g, axis=0, keepdims=True)
    gsum = jnp.sum(jnp.exp(g - gmax), axis=0, keepdims=True)
    p_grp = 1.0 / gsum
    grp = jnp.min(jnp.where(g == gmax, row8, 8), axis=0, keepdims=True)

    e8 = lg_ref[8:16, :]
    for gi in range(1, N_GROUPS):
        e8 = jnp.where(grp == gi, lg_ref[8 + 8 * gi:16 + 8 * gi, :], e8)
    m1 = jnp.max(e8, axis=0, keepdims=True)
    i1 = jnp.min(jnp.where(e8 == m1, row8, 8), axis=0, keepdims=True)
    e8b = jnp.where(row8 == i1, neg, e8)
    m2 = jnp.max(e8b, axis=0, keepdims=True)
    i2 = jnp.min(jnp.where(e8b == m2, row8, 8), axis=0, keepdims=True)
    zsum = jnp.sum(jnp.exp(e8 - m1), axis=0, keepdims=True)
    p1 = 1.0 / zsum
    p2 = jnp.exp(m2 - m1) / zsum
    den = p1 + p2
    w1 = p_grp * p1 / den
    w2 = p_grp * p2 / den
    eid1 = grp * EXPERTS_PER_GROUP + i1
    eid2 = grp * EXPERTS_PER_GROUP + i2

    rowe = lax.broadcasted_iota(jnp.int32, (N_EXPERTS, tm), 0)
    oh1 = (rowe == eid1)
    oh2 = (rowe == eid2)
    tr = lax.broadcasted_iota(jnp.int32, (tm, tm), 0)
    tc = lax.broadcasted_iota(jnp.int32, (tm, tm), 1)
    upper = (tr < tc).astype(BF16)
    cum1 = _dot(oh1.astype(BF16), upper)
    cum2 = _dot(oh2.astype(BF16), upper)
    oh1f = oh1.astype(F32)
    oh2f = oh2.astype(F32)
    tot1 = jnp.sum(oh1f, axis=1, keepdims=True)
    tot2 = jnp.sum(oh2f, axis=1, keepdims=True)
    carry = carry_ref[:, 0:1]
    rank1 = jnp.sum(oh1f * (cum1 + carry), axis=0, keepdims=True)
    rank2 = jnp.sum(oh2f * (cum2 + carry + tot1), axis=0, keepdims=True)
    new_carry = carry + tot1 + tot2
    carry_ref[...] = jnp.broadcast_to(new_carry, carry_ref.shape)
    cnt_ref[...] = jnp.broadcast_to(new_carry, cnt_ref.shape)

    zi = jnp.zeros((1, tm), jnp.int32)
    info_ref[...] = jnp.concatenate(
        [eid1, eid2, rank1.astype(jnp.int32), rank2.astype(jnp.int32), zi, zi, zi, zi], axis=0)
    zf = jnp.zeros((1, tm), F32)
    w_ref[...] = jnp.concatenate([w1, w2, zf, zf, zf, zf, zf, zf], axis=0)


def _route(logits_t, tm=512):
    t = logits_t.shape[1]
    return pl.pallas_call(
        functools.partial(_route_kernel, tm=tm),
        grid=(t // tm,),
        in_specs=[pl.BlockSpec((ROUTER_ROWS, tm), lambda i: (0, i))],
        out_specs=[
            pl.BlockSpec((8, tm), lambda i: (0, i)),
            pl.BlockSpec((8, tm), lambda i: (0, i)),
            pl.BlockSpec((N_EXPERTS, LANES), lambda i: (0, 0)),
        ],
        out_shape=[jax.ShapeDtypeStruct((8, t), jnp.int32), jax.ShapeDtypeStruct((8, t), F32),
                   jax.ShapeDtypeStruct((N_EXPERTS, LANES), F32)],
        scratch_shapes=[pltpu.VMEM((N_EXPERTS, LANES), F32)],
        compiler_params=_cparams(("arbitrary",)),
        name="route",
    )(logits_t)


MOE_BLK = 256


def _dispatch_kernel(zb_ref, dest_ref, h_ref, xs_ref, zero_ref, sem, zsem, *, tm, nb):
    blk_rows = MOE_BLK * ROW_TILE

    @pl.when(pl.program_id(0) == 0)
    def _():
        zero_ref[...] = jnp.zeros_like(zero_ref)

        def zero_copy(b):
            return pltpu.make_async_copy(zero_ref, xs_ref.at[pl.ds(b * blk_rows, blk_rows)], zsem)

        def zstart(b, carry):
            @pl.when(zb_ref[b] == 1)
            def _():
                zero_copy(b).start()
            return carry

        def zwait(b, carry):
            @pl.when(zb_ref[b] == 1)
            def _():
                zero_copy(b).wait()
            return carry

        lax.fori_loop(0, nb, zstart, 0)
        lax.fori_loop(0, nb, zwait, 0)

    def row_copy(t, d):
        src = h_ref.at[pl.ds(pl.multiple_of(t * ROW_TILE, ROW_TILE), ROW_TILE)]
        dst = xs_ref.at[pl.ds(pl.multiple_of(d * ROW_TILE, ROW_TILE), ROW_TILE)]
        return pltpu.make_async_copy(src, dst, sem)

    def issue(t, carry):
        row_copy(t, dest_ref[0, t]).start(priority=0)
        row_copy(t, dest_ref[1, t]).start(priority=1)
        return carry

    lax.fori_loop(0, tm, issue, 0, unroll=8)
    tile_wait = pltpu.make_async_copy(h_ref, xs_ref.at[pl.ds(0, tm * ROW_TILE)], sem)
    tile_wait.wait()
    tile_wait.wait()


def _dispatch(zero_blk, dest, h2p, nb, tm=512):
    t = h2p.shape[0] // ROW_TILE
    grid_spec = pltpu.PrefetchScalarGridSpec(
        num_scalar_prefetch=1,
        grid=(t // tm,),
        in_specs=[
            pl.BlockSpec((8, tm), lambda i, zb: (0, i), memory_space=pltpu.SMEM),
            pl.BlockSpec((tm * ROW_TILE, LANES), lambda i, zb: (i, 0)),
        ],
        out_specs=pl.BlockSpec(memory_space=pl.ANY),
        scratch_shapes=[pltpu.VMEM((MOE_BLK * ROW_TILE, LANES), jnp.uint32), pltpu.SemaphoreType.DMA(()),
                        pltpu.SemaphoreType.DMA(())],
    )
    return pl.pallas_call(
        functools.partial(_dispatch_kernel, tm=tm, nb=nb),
        grid_spec=grid_spec,
        out_shape=jax.ShapeDtypeStruct((nb * MOE_BLK * ROW_TILE, LANES), jnp.uint32),
        compiler_params=_cparams(("arbitrary",)),
        name="dispatch",
    )(zero_blk, dest, h2p)


def _expert_kernel(be_ref, nu_ref, x_ref, wg_ref, wu_ref, wd_ref, y_ref, wgb_ref, wub_ref, wdb_ref, *, d):
    b = pl.program_id(0)

    @pl.when(jnp.logical_or(b == 0, be_ref[b] != be_ref[jnp.maximum(b - 1, 0)]))
    def _():
        wgb_ref[...] = wg_ref[0].astype(BF16)
        wub_ref[...] = wu_ref[0].astype(BF16)
        wdb_ref[...] = wd_ref[0].astype(BF16)

    @pl.when(b < nu_ref[0])
    def _():
        lo, hi = _unpack_pair(_load_rows_tiled(x_ref, MOE_BLK))
        x = jnp.concatenate([lo, hi], axis=1).astype(BF16)
        gate = _dot(x, wgb_ref[...])
        up = _dot(x, wub_ref[...])
        hid = (gate * _sigmoid(gate) * up).astype(BF16)
        y = _dot(hid, wdb_ref[...])
        half = d // 2
        _store_rows_tiled(y_ref, _pack_pair(y[:, :half], y[:, half:]))

    @pl.when(b >= nu_ref[0])
    def _():
        y_ref[...] = jnp.zeros_like(y_ref)


def _experts(blk_expert, n_used, xs, wg, wu, wd, d):
    p = xs.shape[0] // ROW_TILE
    nb = p // MOE_BLK
    blk_rows = MOE_BLK * ROW_TILE

    def xmap(b, be, nu):
        return (jnp.minimum(b, nu[0] - 1), 0)

    grid_spec = pltpu.PrefetchScalarGridSpec(
        num_scalar_prefetch=2,
        grid=(nb,),
        in_specs=[
            pl.BlockSpec((blk_rows, LANES), xmap),
            pl.BlockSpec((1, d, D_EXPERT), lambda b, be, nu: (be[b], 0, 0)),
            pl.BlockSpec((1, d, D_EXPERT), lambda b, be, nu: (be[b], 0, 0)),
            pl.BlockSpec((1, D_EXPERT, d), lambda b, be, nu: (be[b], 0, 0)),
        ],
        out_specs=pl.BlockSpec((blk_rows, LANES), lambda b, be, nu: (b, 0)),
        scratch_shapes=[pltpu.VMEM((d, D_EXPERT), BF16), pltpu.VMEM((d, D_EXPERT), BF16),
                        pltpu.VMEM((D_EXPERT, d), BF16)],
    )
    return pl.pallas_call(
        functools.partial(_expert_kernel, d=d),
        grid_spec=grid_spec,
        out_shape=jax.ShapeDtypeStruct((p * ROW_TILE, LANES), jnp.uint32),
        compiler_params=_cparams(("arbitrary",)),
        name="experts",
    )(blk_expert, n_used, xs, wg, wu, wd)


COMBINE_GROUP = 8


def _combine_kernel(dcur_ref, dnxt_ref, x1_ref, wt_ref, yb_ref, op_ref, os_ref, buf_ref, sem, *, tm, npb, nsteps):
    i = pl.program_id(0)
    slot = lax.rem(i, 2)
    nslot = 1 - slot
    grp = COMBINE_GROUP

    def row_copy(s, k, t, d):
        src = yb_ref.at[pl.ds(pl.multiple_of(d * ROW_TILE, ROW_TILE), ROW_TILE)]
        dst = buf_ref.at[s, k, pl.ds(pl.multiple_of(t * ROW_TILE, ROW_TILE), ROW_TILE)]
        return pltpu.make_async_copy(src, dst, sem.at[s])

    def issue_group(dref, s, t0):
        for u in range(grp):
            row_copy(s, 0, t0 + u, dref[0, t0 + u]).start(priority=0)
            row_copy(s, 1, t0 + u, dref[1, t0 + u]).start(priority=1)

    def wait_slot(s):
        for k in range(TOP_K):
            pltpu.make_async_copy(yb_ref.at[pl.ds(0, tm * ROW_TILE)], buf_ref.at[s, k], sem.at[s]).wait()

    @pl.when(i == 0)
    def _():
        def first(g, carry):
            issue_group(dcur_ref, 0, g * grp)
            return carry

        lax.fori_loop(0, tm // grp, first, 0)

    wait_slot(slot)

    def run(o_ref):
        def group(g, carry):
            t0 = pl.multiple_of(g * grp, grp)
            issue_group(dnxt_ref, nslot, t0)
            rows = pl.ds(t0, grp)
            w = wt_ref[rows, :]
            w1 = w[:, 0:1]
            w2 = w[:, 1:2]
            a_lo, a_hi = _unpack_pair(_load_rows_tiled(buf_ref, grp, t0, lead=(slot, 0)))
            b_lo, b_hi = _unpack_pair(_load_rows_tiled(buf_ref, grp, t0, lead=(slot, 1)))
            moe = jnp.concatenate([a_lo * w1 + b_lo * w2, a_hi * w1 + b_hi * w2], axis=1)
            o_ref[rows, :] = x1_ref[rows, :] + moe
            return carry

        lax.fori_loop(0, tm // grp, group, 0)

    @pl.when(i < npb)
    def _():
        run(op_ref)

    @pl.when(i >= npb)
    def _():
        run(os_ref)

    @pl.when(i == nsteps - 1)
    def _():
        wait_slot(nslot)


def _combine(dest, x1, wt, yb, tp, tm=256):
    t, d = x1.shape
    npb = tp // tm
    nsteps = t // tm
    return pl.pallas_call(
        functools.partial(_combine_kernel, tm=tm, npb=npb, nsteps=nsteps),
        grid=(nsteps,),
        in_specs=[
            pl.BlockSpec((8, tm), lambda i: (0, i), memory_space=pltpu.SMEM),
            pl.BlockSpec((8, tm), lambda i: (0, jnp.minimum(i + 1, nsteps - 1)), memory_space=pltpu.SMEM),
            pl.BlockSpec((tm, d), lambda i: (i, 0)),
            pl.BlockSpec((tm, 8), lambda i: (i, 0)),
            pl.BlockSpec(memory_space=pl.ANY),
        ],
        out_specs=[
            pl.BlockSpec((tm, d), lambda i: (jnp.minimum(i, npb - 1), 0)),
            pl.BlockSpec((tm, d), lambda i: (jnp.maximum(i - npb, 0), 0)),
        ],
        out_shape=[jax.ShapeDtypeStruct((tp, d), F32), jax.ShapeDtypeStruct((t - tp, d), F32)],
        scratch_shapes=[pltpu.VMEM((2, TOP_K, tm * ROW_TILE, LANES), jnp.uint32), pltpu.SemaphoreType.DMA((2,))],
        compiler_params=_cparams(("arbitrary",)),
        name="combine",
    )(dest, dest, x1, wt, yb)


def _rope_tables(length):
    rows = length // GRID_W
    row = jnp.broadcast_to(jnp.arange(rows, dtype=F32)[:, None], (rows, GRID_W)).reshape(length)
    col = jnp.broadcast_to(jnp.arange(GRID_W, dtype=F32)[None, :], (rows, GRID_W)).reshape(length)
    axis_dim = HEAD_DIM // 2
    freqs = ROPE_THETA ** (-jnp.arange(0, axis_dim, 2, dtype=F32) / axis_dim)
    ang = jnp.concatenate([row[:, None] * freqs, col[:, None] * freqs], axis=-1)
    c, s = jnp.cos(ang), jnp.sin(ang)
    return jnp.concatenate([c, c], axis=1), jnp.concatenate([-s, s], axis=1)


def _layer(xp, xs, mem_all, seqs, n_mem, p):
    d = xp.shape[1]
    tp = xp.shape[0]
    lay = _proj_layout(d)

    w_in = p["w_in"]
    sizes = (GLA_KEY, GLA_KEY, GLA_VAL, GLA_VAL, GLA_RANK, GLA_RANK, GQA_Q, GQA_KV, GQA_KV, MEM_W, 3 * d)
    offs = np.concatenate([[0], np.cumsum(sizes)])
    seg = {n: w_in[:, offs[k]:offs[k + 1]] for k, n in enumerate(
        ("gq", "gk", "gv", "gg", "ga_f", "ga_b", "aq", "ak", "av", "mq", "mg"))}
    def deinterleave(w, heads):
        rows = w.shape[0]
        return w.reshape(rows, heads, HEAD_DIM // 2, 2).transpose(0, 1, 3, 2).reshape(rows, heads * HEAD_DIM)

    w_main = jnp.concatenate([seg["mg"], seg["gv"], seg["gg"], deinterleave(seg["aq"], GQA_HEADS), seg["mq"],
                              seg["gq"], seg["gk"], deinterleave(seg["ak"], GQA_KV_HEADS), seg["av"]],
                             axis=1).astype(BF16)
    w_ga = jnp.concatenate([seg["ga_f"], seg["ga_b"], jnp.zeros((d, LANES - 2 * GLA_RANK), F32)],
                           axis=1).astype(BF16)

    def a2_pad(a2, row0):
        pad = [jnp.zeros((row0, GLA_KEY), F32)] if row0 else []
        pad += [a2, jnp.zeros((LANES - row0 - GLA_RANK, GLA_KEY), F32)]
        return jnp.concatenate(pad, axis=0).astype(BF16)

    gq_perm = deinterleave(p["g_q_gqa"][None, :], 1)
    gk_perm = deinterleave(p["g_k_gqa"][None, :], 1)

    proj, ga = _inproj(xp, xs, p["g_mix"][None, :], w_main, w_ga)

    o_f = _gla(proj, ga, a2_pad(p["gla_a2_fwd"], 0), p["gla_ab_fwd"][None, :], lay, seqs, False)
    o_b = _gla(proj, ga, a2_pad(p["gla_a2_bwd"], GLA_RANK), p["gla_ab_bwd"][None, :], lay, seqs, True)

    max_len = max(length for _, length in seqs)
    cos_t, sin_t = _rope_tables(max_len)
    qt3, ka, vt3 = _qkprep(proj, cos_t, sin_t, gq_perm, gk_perm, lay, seqs)
    groups = []
    for (start, length) in seqs:
        if groups and groups[-1][2] == length and groups[-1][0] + groups[-1][1] * length == start:
            groups[-1][1] += 1
        else:
            groups.append([start, 1, length])
    o_gqa = jnp.zeros((proj.shape[0], GQA_Q), BF16)
    for (start, nseq, length) in groups:
        o_gqa = _flash_group(qt3, ka, vt3, start, nseq, length, o_gqa)

    km, vm = _memkv(mem_all, p["g_mem_norm"][None, :], p["w_mem_kv"].astype(BF16), p["g_k_mem"][None, :])
    o_mem = _memattn(proj, km, vm, p["g_q_mem"][None, :], lay, seqs, n_mem)

    merged = _merge(o_f, o_b, proj, o_gqa, o_mem, p["g_gla_out"][None, :], p["w_br_gla"].astype(BF16),
                    p["w_br_gqa"].astype(BF16), p["w_br_mem"].astype(BF16), lay, d)
    gpad = 8 - N_GROUPS
    wr = jnp.concatenate([p["w_router_group"].T, jnp.zeros((gpad, d), F32), p["w_router_expert"].T], axis=0)
    wr_hi = wr.astype(BF16)
    wr_lo = (wr - wr_hi.astype(F32)).astype(BF16)
    br = jnp.concatenate([p["b_router_group"], jnp.zeros((gpad,), F32), p["b_router_expert"]])
    br = jnp.broadcast_to(br[:, None], (ROUTER_ROWS, LANES))
    x1, h2p, logits_t = _outproj(merged, xp, xs, p["w_out"].astype(BF16), p["g_ffn"][None, :], wr_hi, wr_lo, br)

    info, wts, counts = _route(logits_t)
    t = x1.shape[0]
    cnt = counts[:, 0].astype(jnp.int32)
    padded = ((cnt + MOE_BLK - 1) // MOE_BLK) * MOE_BLK
    pad_ends = jnp.cumsum(padded)
    pad_starts = pad_ends - padded
    nb = (t * TOP_K) // MOE_BLK + N_EXPERTS
    n_used = (pad_ends[-1] // MOE_BLK).astype(jnp.int32)
    blk = jnp.arange(nb, dtype=jnp.int32)
    first_row = jnp.minimum(blk, n_used - 1) * MOE_BLK
    blk_expert = jnp.sum((pad_ends[None, :] <= first_row[:, None]).astype(jnp.int32), axis=1)
    blk_expert = jnp.minimum(blk_expert, N_EXPERTS - 1)
    ragged = jnp.logical_and(cnt > 0, cnt % MOE_BLK != 0)
    is_last = jnp.any(jnp.logical_and((pad_ends // MOE_BLK - 1)[None, :] == blk[:, None], ragged[None, :]), axis=1)
    zero_blk = jnp.logical_or(is_last, blk >= n_used).astype(jnp.int32)
    eid = info[0:2]
    sel = eid[None] == jnp.arange(N_EXPERTS, dtype=jnp.int32)[:, None, None]
    dest = info[2:4] + jnp.sum(jnp.where(sel, pad_starts[:, None, None], 0), axis=0)
    dest = jnp.concatenate([dest, jnp.zeros((6, t), jnp.int32)], axis=0)

    xs_sorted = _dispatch(zero_blk, dest, h2p, nb)
    yb = _experts(blk_expert, n_used[None], xs_sorted, p["w_exp_gate"], p["w_exp_up"], p["w_exp_down"], d)
    return _combine(dest, x1, wts.T, yb, tp)


def kernel(x_prompt, x_sample, mem_prompt, mem_sample, g_mix, w_in, gla_a2_fwd, gla_ab_fwd, gla_a2_bwd, gla_ab_bwd, g_gla_out, g_q_gqa, g_k_gqa, g_mem_norm, w_mem_kv, g_q_mem, g_k_mem, w_br_gla, w_br_gqa, w_br_mem, w_out, g_ffn, w_router_group, b_router_group, w_router_expert, b_router_expert, w_exp_gate, w_exp_up, w_exp_down):
    bp, lp, d = x_prompt.shape
    bs, ls, _ = x_sample.shape
    n_mem = mem_prompt.shape[1]
    depth = g_mix.shape[0]
    seqs = [(b * lp, lp) for b in range(bp)] + [(bp * lp + b * ls, ls) for b in range(bs)]
    xp = x_prompt.reshape(bp * lp, d)
    xs = x_sample.reshape(bs * ls, d)
    mem_all = jnp.concatenate([mem_prompt.reshape(bp * n_mem, d), mem_sample.reshape(bs * n_mem, d)], axis=0)
    names = ("g_mix", "w_in", "gla_a2_fwd", "gla_ab_fwd", "gla_a2_bwd", "gla_ab_bwd", "g_gla_out", "g_q_gqa",
             "g_k_gqa", "g_mem_norm", "w_mem_kv", "g_q_mem", "g_k_mem", "w_br_gla", "w_br_gqa", "w_br_mem",
             "w_out", "g_ffn", "w_router_group", "b_router_group", "w_router_expert", "b_router_expert",
             "w_exp_gate", "w_exp_up", "w_exp_down")
    vals = (g_mix, w_in, gla_a2_fwd, gla_ab_fwd, gla_a2_bwd, gla_ab_bwd, g_gla_out, g_q_gqa, g_k_gqa,
            g_mem_norm, w_mem_kv, g_q_mem, g_k_mem, w_br_gla, w_br_gqa, w_br_mem, w_out, g_ffn,
            w_router_group, b_router_group, w_router_expert, b_router_expert, w_exp_gate, w_exp_up, w_exp_down)
    for layer in range(depth):
        p = {n: v[layer] for n, v in zip(names, vals)}
        xp, xs = _layer(xp, xs, mem_all, seqs, n_mem, p)
    return xp.reshape(bp, lp, d), xs.reshape(bs, ls, d)
```

```python
import functools

import numpy as np
import jax
import jax.numpy as jnp
from jax import lax
from jax.experimental import pallas as pl
from jax.experimental.pallas import tpu as pltpu

F32 = jnp.float32
BF16 = jnp.bfloat16

GRID_W = 64
GLA_HEADS, GLA_DK, GLA_DV = 4, 128, 256
GLA_KEY, GLA_VAL = GLA_HEADS * GLA_DK, GLA_HEADS * GLA_DV
GLA_RANK, GLA_TAU, GLA_CHUNK = 16, 16.0, 64
GQA_HEADS, GQA_KV_HEADS, HEAD_DIM = 8, 2, 128
GQA_GROUP = GQA_HEADS // GQA_KV_HEADS
GQA_Q, GQA_KV = GQA_HEADS * HEAD_DIM, GQA_KV_HEADS * HEAD_DIM
ROPE_THETA = 10000.0
MEM_HEADS, MEM_HD = 4, 256
MEM_W = MEM_HEADS * MEM_HD
N_GROUPS, EXPERTS_PER_GROUP, TOP_K, D_EXPERT = 4, 8, 2, 512
N_EXPERTS = N_GROUPS * EXPERTS_PER_GROUP
EPS = 1e-6

LANES = 128
VMEM_LIMIT_BYTES = 56 * 1024 * 1024


def _proj_layout(d):
    off = {}
    c = 0
    for name, w in (("mg", 3 * d), ("gv", GLA_VAL), ("gg", GLA_VAL), ("aq", GQA_Q), ("mq", MEM_W),
                    ("gq", GLA_KEY), ("gk", GLA_KEY), ("ak", GQA_KV), ("av", GQA_KV)):
        off[name] = c
        c += w
    off["total"] = c
    return off


def _cparams(sem, vmem=VMEM_LIMIT_BYTES):
    return pltpu.CompilerParams(dimension_semantics=sem, vmem_limit_bytes=vmem)


def _rms(x, g):
    ms = jnp.mean(x * x, axis=-1, keepdims=True)
    return x * lax.rsqrt(ms + EPS) * g


def _dot(a, b):
    return jnp.dot(a, b, preferred_element_type=F32)


def _dot_nt(a, b):
    return lax.dot_general(a, b, (((1,), (1,)), ((), ())), preferred_element_type=F32)


def _dot_tn(a, b):
    return lax.dot_general(a, b, (((0,), (0,)), ((), ())), preferred_element_type=F32)


def _sigmoid(x):
    return 1.0 / (1.0 + jnp.exp(-x))


def _pack_pair(lo, hi):
    lo_b = pltpu.bitcast(lo.astype(BF16).astype(F32), jnp.uint32)
    hi_b = pltpu.bitcast(hi.astype(BF16).astype(F32), jnp.uint32)
    return (hi_b & jnp.uint32(0xFFFF0000)) | (lo_b >> 16)


def _unpack_pair(w):
    lo = pltpu.bitcast(w << 16, F32)
    hi = pltpu.bitcast(w & jnp.uint32(0xFFFF0000), F32)
    return lo, hi


ROW_TILE = 8
PAIR_CHUNK = 2 * LANES


def _chunk_rows(n, chunk, start=0):
    return pl.ds(start * ROW_TILE + chunk, n, stride=ROW_TILE)


def _pack_chunk(x):
    return _pack_pair(x[:, :LANES], x[:, LANES:])


def _unpack_chunk(w):
    lo, hi = _unpack_pair(w)
    return jnp.concatenate([lo, hi], axis=1)


def _inproj_kernel(xp_ref, xs_ref, g_ref, w_ref, wga_ref, o_ref, ga_ref, h_ref, *, npb):
    i = pl.program_id(0)
    j = pl.program_id(1)

    def norm(x_ref):
        hb = _rms(x_ref[...], g_ref[...]).astype(BF16)
        h_ref[...] = hb
        ga_ref[...] = _dot(hb, wga_ref[...])

    @pl.when(jnp.logical_and(j == 0, i < npb))
    def _():
        norm(xp_ref)

    @pl.when(jnp.logical_and(j == 0, i >= npb))
    def _():
        norm(xs_ref)

    o_ref[...] = _dot(h_ref[...], w_ref[...]).astype(BF16)


def _inproj(xp, xs, g, w_main, w_ga, tm=1024, tn=512):
    tp, d = xp.shape
    ts = xs.shape[0]
    t = tp + ts
    nc = w_main.shape[1]
    npb = tp // tm
    grid = (t // tm, nc // tn)
    return pl.pallas_call(
        functools.partial(_inproj_kernel, npb=npb),
        grid=grid,
        in_specs=[
            pl.BlockSpec((tm, d), lambda i, j: (jnp.minimum(i, npb - 1), 0)),
            pl.BlockSpec((tm, d), lambda i, j: (jnp.maximum(i - npb, 0), 0)),
            pl.BlockSpec((1, d), lambda i, j: (0, 0)),
            pl.BlockSpec((d, tn), lambda i, j: (0, j)),
            pl.BlockSpec((d, LANES), lambda i, j: (0, 0)),
        ],
        out_specs=[
            pl.BlockSpec((tm, tn), lambda i, j: (i, j)),
            pl.BlockSpec((tm, LANES), lambda i, j: (i, 0)),
        ],
        out_shape=[jax.ShapeDtypeStruct((t, nc), BF16), jax.ShapeDtypeStruct((t, LANES), F32)],
        scratch_shapes=[pltpu.VMEM((tm, d), BF16)],
        compiler_params=_cparams(("arbitrary", "arbitrary")),
        name="inproj",
    )(xp, xs, g, w_main, w_ga)


GLA_TB = 512


def _gla_kernel(rbf_ref, rbb_ref, first_ref,
                qf_ref, kf_ref, vf_ref, gaf_ref, qb_ref, kb_ref, vb_ref, gab_ref,
                a2f_ref, abf_ref, a2b_ref, abb_ref, of_ref, ob_ref, sf_ref, sb_ref):
    @pl.when(first_ref[pl.program_id(0)] == 1)
    def _():
        sf_ref[...] = jnp.zeros_like(sf_ref)
        sb_ref[...] = jnp.zeros_like(sb_ref)

    _gla_direction(qf_ref, kf_ref, vf_ref, gaf_ref, a2f_ref, abf_ref, of_ref, sf_ref, False)
    _gla_direction(qb_ref, kb_ref, vb_ref, gab_ref, a2b_ref, abb_ref, ob_ref, sb_ref, True)


def _gla_direction(q_ref, k_ref, v_ref, ga_ref, a2_ref, ab_ref, o_ref, s_ref, reverse):
    tb, c = GLA_TB, GLA_CHUNK
    nch = tb // c

    r = lax.broadcasted_iota(jnp.int32, (tb, tb), 0)
    cc = lax.broadcasted_iota(jnp.int32, (tb, tb), 1)
    same = (r >> 6) == (cc >> 6)
    if reverse:
        tri = (cc >= r).astype(BF16)
        amask = jnp.logical_and(same, cc > r)
    else:
        tri = (cc <= r).astype(BF16)
        amask = jnp.logical_and(same, cc <= r)

    z = _dot(ga_ref[...].astype(BF16), a2_ref[...]) + ab_ref[...]
    la = (jnp.minimum(z, 0.0) - jnp.log1p(jnp.exp(-jnp.abs(z)))) * (1.0 / GLA_TAU)
    la_hi = la.astype(BF16)
    la_lo = (la - la_hi.astype(F32)).astype(BF16)
    pre_all = _dot(tri, la_hi) + _dot(tri, la_lo)
    scale = GLA_DK ** -0.5
    for h in range(GLA_HEADS):
        pre = pre_all[:, h * GLA_DK:(h + 1) * GLA_DK]
        bs, tots = [], []
        for ci in range(nch):
            r0 = ci * c
            if reverse:
                base = pre[r0 + c:r0 + c + 1] if ci < nch - 1 else jnp.zeros((1, GLA_DK), F32)
                tot = pre[r0:r0 + 1] - base
            else:
                base = pre[r0 - 1:r0] if ci > 0 else jnp.zeros((1, GLA_DK), F32)
                tot = pre[r0 + c - 1:r0 + c] - base
            bs.append(pre[r0:r0 + c] - base)
            tots.append(tot)
        b = jnp.concatenate(bs, axis=0)
        btot = jnp.concatenate([jnp.broadcast_to(tt, (c, GLA_DK)) for tt in tots], axis=0)

        q = q_ref[:, h * GLA_DK:(h + 1) * GLA_DK].astype(F32) * scale
        k = k_ref[:, h * GLA_DK:(h + 1) * GLA_DK].astype(F32)
        v = v_ref[:, h * GLA_DV:(h + 1) * GLA_DV]
        q_e = (q * jnp.exp(b)).astype(BF16)
        k_e = (k * jnp.exp(-b)).astype(BF16)
        k_d = (k * jnp.exp(btot - b)).astype(BF16)
        att = jnp.where(amask, _dot_nt(q_e, k_e), 0.0).astype(BF16)
        o_intra = _dot(att, v)

        st = s_ref[h]
        outs = [None] * nch
        order = range(nch - 1, -1, -1) if reverse else range(nch)
        for ci in order:
            r0 = ci * c
            o_inter = _dot_nt(q_e[r0:r0 + c], st.astype(BF16))
            outs[ci] = o_intra[r0:r0 + c] + o_inter
            kvt = _dot_tn(v[r0:r0 + c], k_d[r0:r0 + c])
            st = st * jnp.exp(tots[ci]) + kvt
        s_ref[h] = st
        o_ref[:, h * GLA_DV:(h + 1) * GLA_DV] = jnp.concatenate(outs, axis=0).astype(BF16)


def _gla(proj, ga, a2f, abf, a2b, abb, lay, seqs):
    t = proj.shape[0]
    tb = GLA_TB
    rbf, rbb, first = [], [], []
    for (start, length) in seqs:
        blocks = list(range(start // tb, (start + length) // tb))
        rbf += blocks
        rbb += blocks[::-1]
        first += [1] + [0] * (len(blocks) - 1)
    rbf, rbb, first = (jnp.asarray(np.array(a, np.int32)) for a in (rbf, rbb, first))
    qb, kb, vb = lay["gq"] // GLA_KEY, lay["gk"] // GLA_KEY, lay["gv"] // GLA_VAL

    def data_specs(pick):
        return [
            pl.BlockSpec((tb, GLA_KEY), lambda j, rf, rb, fr: (pick(rf, rb)[j], qb)),
            pl.BlockSpec((tb, GLA_KEY), lambda j, rf, rb, fr: (pick(rf, rb)[j], kb)),
            pl.BlockSpec((tb, GLA_VAL), lambda j, rf, rb, fr: (pick(rf, rb)[j], vb)),
            pl.BlockSpec((tb, LANES), lambda j, rf, rb, fr: (pick(rf, rb)[j], 0)),
        ]

    def const2(shape):
        return pl.BlockSpec(shape, lambda j, rf, rb, fr: (0, 0))

    fwd = lambda rf, rb: rf
    bwd = lambda rf, rb: rb
    grid_spec = pltpu.PrefetchScalarGridSpec(
        num_scalar_prefetch=3,
        grid=(t // tb,),
        in_specs=data_specs(fwd) + data_specs(bwd) + [
            const2((LANES, GLA_KEY)), const2((1, GLA_KEY)), const2((LANES, GLA_KEY)), const2((1, GLA_KEY))],
        out_specs=[pl.BlockSpec((tb, GLA_VAL), lambda j, rf, rb, fr: (rf[j], 0)),
                   pl.BlockSpec((tb, GLA_VAL), lambda j, rf, rb, fr: (rb[j], 0))],
        scratch_shapes=[pltpu.VMEM((GLA_HEADS, GLA_DV, GLA_DK), F32),
                        pltpu.VMEM((GLA_HEADS, GLA_DV, GLA_DK), F32)],
    )
    return pl.pallas_call(
        _gla_kernel,
        grid_spec=grid_spec,
        out_shape=[jax.ShapeDtypeStruct((t, GLA_VAL), BF16), jax.ShapeDtypeStruct((t, GLA_VAL), BF16)],
        compiler_params=_cparams(("arbitrary",)),
        name="gla",
    )(rbf, rbb, first, proj, proj, proj, ga, proj, proj, proj, ga, a2f, abf, a2b, abb)


FLASH_TQ = 512
FLASH_TK = 512
LOG2E = 1.4426950408889634
BF16_SUBLANES = 16
VT_ROWS = HEAD_DIM + BF16_SUBLANES


def _qkprep_kernel(pb_ref, aq_ref, ak_ref, av_ref, cos_ref, sin_ref, gq_ref, gk_ref, qt_ref, ka_ref, vt_ref):
    cosv = cos_ref[...]
    sinv = sin_ref[...]
    scale = HEAD_DIM ** -0.5 * LOG2E
    nsub = FLASH_TK // FLASH_TQ

    def one(x, g):
        xn = _rms(x.astype(F32), g)
        return xn * cosv + pltpu.roll(xn, HEAD_DIM // 2, 1) * sinv

    for h in range(GQA_HEADS):
        sl = slice(h * HEAD_DIM, (h + 1) * HEAD_DIM)
        qt = (one(aq_ref[:, sl], gq_ref[...]) * scale).T.astype(BF16)
        for u in range(nsub):
            qt_ref[u, sl, :] = qt[:, u * FLASH_TQ:(u + 1) * FLASH_TQ]
    for h in range(GQA_KV_HEADS):
        sl = slice(h * HEAD_DIM, (h + 1) * HEAD_DIM)
        ka_ref[:, sl] = one(ak_ref[:, sl], gk_ref[...]).astype(BF16)
        vt_ref[0, h * VT_ROWS:h * VT_ROWS + HEAD_DIM, :] = av_ref[:, sl].astype(F32).T.astype(BF16)
        vt_ref[0, h * VT_ROWS + HEAD_DIM:(h + 1) * VT_ROWS, :] = jnp.ones((BF16_SUBLANES, FLASH_TK), BF16)


def _qkprep(proj, cos_t, sin_t, gq, gk, lay, seqs):
    t = proj.shape[0]
    tm = FLASH_TK
    nsub = FLASH_TK // FLASH_TQ
    pb = []
    for (start, length) in seqs:
        pb += list(range(length // tm))
    pb = jnp.asarray(np.array(pb, np.int32))
    aqb, akb = lay["aq"] // GQA_Q, lay["ak"] // GQA_KV
    grid_spec = pltpu.PrefetchScalarGridSpec(
        num_scalar_prefetch=1,
        grid=(t // tm,),
        in_specs=[
            pl.BlockSpec((tm, GQA_Q), lambda i, p: (i, aqb)),
            pl.BlockSpec((tm, GQA_KV), lambda i, p: (i, akb)),
            pl.BlockSpec((tm, GQA_KV), lambda i, p: (i, akb + 1)),
            pl.BlockSpec((tm, HEAD_DIM), lambda i, p: (p[i], 0)),
            pl.BlockSpec((tm, HEAD_DIM), lambda i, p: (p[i], 0)),
            pl.BlockSpec((1, HEAD_DIM), lambda i, p: (0, 0)),
            pl.BlockSpec((1, HEAD_DIM), lambda i, p: (0, 0)),
        ],
        out_specs=[
            pl.BlockSpec((nsub, GQA_Q, FLASH_TQ), lambda i, p: (i, 0, 0)),
            pl.BlockSpec((tm, GQA_KV), lambda i, p: (i, 0)),
            pl.BlockSpec((1, GQA_KV_HEADS * VT_ROWS, tm), lambda i, p: (i, 0, 0)),
        ],
    )
    return pl.pallas_call(
        _qkprep_kernel,
        grid_spec=grid_spec,
        out_shape=[jax.ShapeDtypeStruct((t // FLASH_TQ, GQA_Q, FLASH_TQ), BF16),
                   jax.ShapeDtypeStruct((t, GQA_KV), BF16),
                   jax.ShapeDtypeStruct((t // tm, GQA_KV_HEADS * VT_ROWS, tm), BF16)],
        compiler_params=_cparams(("arbitrary",)),
        name="qkprep",
    )(pb, proj, proj, proj, cos_t, sin_t, gq, gk)


FLASH_REFRAME = 64.0


def _flash_kernel(q_ref, k_ref, v_ref, prev_ref, o_ref, acc_ref, p0_ref, p1_ref, frame_ref, mcur_ref, *, nk):
    del prev_ref
    tq, tk = FLASH_TQ, FLASH_TK
    qt = jnp.concatenate([q_ref[0, g * HEAD_DIM:(g + 1) * HEAD_DIM, :] for g in range(GQA_GROUP)], axis=1)

    def scores(ci):
        off = pl.multiple_of(ci * tk, tk)
        return _dot(k_ref[pl.ds(off, tk), :], qt)

    def reframe(ci):
        c_new = jnp.maximum(frame_ref[...], mcur_ref[...])
        alpha = jnp.exp2(frame_ref[...] - c_new)
        pt = jnp.exp2(scores(ci) - c_new).astype(BF16)
        acc_ref[...] = alpha * acc_ref[...] + _dot(v_ref[ci], pt)
        frame_ref[...] = c_new

    st0 = scores(0)
    frame_ref[...] = jnp.max(st0, axis=0, keepdims=True)
    acc_ref[...] = _dot(v_ref[0], jnp.exp2(st0 - frame_ref[...]).astype(BF16))
    p_refs = (p0_ref, p1_ref)
    p0_ref[...] = jnp.zeros(p0_ref.shape, BF16)

    def step(ci, par, need_prev):
        cur_ref, prv_ref = p_refs[par], p_refs[1 - par]

        @pl.when(need_prev)
        def _():
            reframe(ci - 1)
            prv_ref[...] = jnp.zeros(prv_ref.shape, BF16)

        st = scores(ci)
        m_cur = jnp.max(st, axis=0, keepdims=True)
        mcur_ref[...] = m_cur
        cur_ref[...] = jnp.exp2(st - frame_ref[...]).astype(BF16)
        acc_ref[...] += _dot(v_ref[ci - 1], prv_ref[...])
        return jnp.max(m_cur - frame_ref[...]) > FLASH_REFRAME

    def pair(j, need_prev):
        ci = 1 + 2 * j
        return step(ci + 1, 0, step(ci, 1, need_prev))

    need_last = lax.fori_loop(0, (nk - 1) // 2, pair, False)
    if (nk - 1) % 2:
        need_last = step(nk - 1, (nk - 1) % 2, need_last)
    last = nk - 1

    @pl.when(need_last)
    def _():
        reframe(last)

    @pl.when(jnp.logical_not(need_last))
    def _():
        acc_ref[...] += _dot(v_ref[last], p_refs[last % 2][...])

    out_t = acc_ref[0:HEAD_DIM, :] / acc_ref[HEAD_DIM:HEAD_DIM + 1, :]
    for g in range(GQA_GROUP):
        o_ref[:, g * HEAD_DIM:(g + 1) * HEAD_DIM] = out_t[:, g * tq:(g + 1) * tq].T.astype(BF16)


def _flash_group(qt3, ka, vt3, start, nseq, length, prev):
    t = ka.shape[0]
    tq, tk = FLASH_TQ, FLASH_TK
    nqt = length // tq
    nk = length // tk
    qb0 = start // tq
    sb0 = start // length
    gw = GQA_GROUP * HEAD_DIM
    in_specs = [
        pl.BlockSpec((1, gw, tq), lambda s, kv, i: (qb0 + s * nqt + i, kv, 0)),
        pl.BlockSpec((length, HEAD_DIM), lambda s, kv, i: (sb0 + s, kv)),
        pl.BlockSpec((nk, VT_ROWS, tk), lambda s, kv, i: (sb0 + s, kv, 0)),
        pl.BlockSpec(memory_space=pl.ANY),
    ]
    return pl.pallas_call(
        functools.partial(_flash_kernel, nk=nk),
        grid=(nseq, GQA_KV_HEADS, nqt),
        in_specs=in_specs,
        out_specs=pl.BlockSpec((tq, gw), lambda s, kv, i: (qb0 + s * nqt + i, kv)),
        out_shape=jax.ShapeDtypeStruct((t, GQA_Q), BF16),
        scratch_shapes=[pltpu.VMEM((VT_ROWS, GQA_GROUP * tq), F32),
                        pltpu.VMEM((tk, GQA_GROUP * tq), BF16),
                        pltpu.VMEM((tk, GQA_GROUP * tq), BF16),
                        pltpu.VMEM((1, GQA_GROUP * tq), F32),
                        pltpu.VMEM((1, GQA_GROUP * tq), F32)],
        input_output_aliases={3: 0},
        compiler_params=_cparams(("arbitrary", "arbitrary", "arbitrary")),
        name="gqa_flash",
    )(qt3, ka, vt3, prev)


def _memkv_kernel(mem_ref, g_ref, w_ref, gk_ref, km_ref, vm_ref):
    hb = _rms(mem_ref[...], g_ref[...]).astype(BF16)
    kv = _dot(hb, w_ref[...])
    for h in range(MEM_HEADS):
        sl = slice(h * MEM_HD, (h + 1) * MEM_HD)
        km_ref[:, sl] = _rms(kv[:, sl], gk_ref[...]).astype(BF16)
    vm_ref[...] = kv[:, MEM_W:].astype(BF16)


def _memkv(mem, g, w, gk):
    rows, d = mem.shape
    m = 256
    return pl.pallas_call(
        _memkv_kernel,
        grid=(rows // m,),
        in_specs=[
            pl.BlockSpec((m, d), lambda i: (i, 0)),
            pl.BlockSpec((1, d), lambda i: (0, 0)),
            pl.BlockSpec((d, 2 * MEM_W), lambda i: (0, 0)),
            pl.BlockSpec((1, MEM_HD), lambda i: (0, 0)),
        ],
        out_specs=[pl.BlockSpec((m, MEM_W), lambda i: (i, 0)), pl.BlockSpec((m, MEM_W), lambda i: (i, 0))],
        out_shape=[jax.ShapeDtypeStruct((rows, MEM_W), BF16), jax.ShapeDtypeStruct((rows, MEM_W), BF16)],
        compiler_params=_cparams(("arbitrary",)),
        name="memkv",
    )(mem, g, w, gk)


def _memattn_kernel(sq_ref, mq_ref, km_ref, vm_ref, gq_ref, o_ref):
    scale = MEM_HD ** -0.5
    for h in range(MEM_HEADS):
        sl = slice(h * MEM_HD, (h + 1) * MEM_HD)
        qn = (_rms(mq_ref[:, sl].astype(F32), gq_ref[...]) * scale).astype(BF16)
        s = _dot_nt(qn, km_ref[:, sl])
        m = jnp.max(s, axis=1, keepdims=True)
        p = jnp.exp(s - m)
        l = jnp.sum(p, axis=1, keepdims=True)
        o_ref[:, sl] = (_dot(p.astype(BF16), vm_ref[:, sl]) / l).astype(BF16)


def _memattn(proj, km, vm, gq, lay, seqs, n_mem, tm=512):
    t = proj.shape[0]
    sq = []
    for si, (start, length) in enumerate(seqs):
        sq += [si] * (length // tm)
    sq = jnp.asarray(np.array(sq, np.int32))
    mqb = lay["mq"] // MEM_W
    grid_spec = pltpu.PrefetchScalarGridSpec(
        num_scalar_prefetch=1,
        grid=(t // tm,),
        in_specs=[
            pl.BlockSpec((tm, MEM_W), lambda i, s: (i, mqb)),
            pl.BlockSpec((n_mem, MEM_W), lambda i, s: (s[i], 0)),
            pl.BlockSpec((n_mem, MEM_W), lambda i, s: (s[i], 0)),
            pl.BlockSpec((1, MEM_HD), lambda i, s: (0, 0)),
        ],
        out_specs=pl.BlockSpec((tm, MEM_W), lambda i, s: (i, 0)),
    )
    return pl.pallas_call(
        _memattn_kernel,
        grid_spec=grid_spec,
        out_shape=jax.ShapeDtypeStruct((t, MEM_W), BF16),
        compiler_params=_cparams(("arbitrary",)),
        name="memattn",
    )(sq, proj, km, vm, gq)


def _merge_kernel(of_ref, ob_ref, gg_ref, og_ref, om_ref, mg_ref, ggla_ref, w0_ref, w1_ref, w2_ref, o_ref, *, d):
    o = of_ref[...].astype(F32) + ob_ref[...].astype(F32)
    gg = gg_ref[...].astype(F32)
    parts = []
    for h in range(GLA_HEADS):
        sl = slice(h * GLA_DV, (h + 1) * GLA_DV)
        gh = gg[:, sl]
        parts.append((_rms(o[:, sl], ggla_ref[...]) * (gh * _sigmoid(gh))).astype(BF16))
    y0 = jnp.concatenate(parts, axis=1)
    acc = _sigmoid(mg_ref[:, 0:d].astype(F32)) * _dot(y0, w0_ref[...])
    acc = acc + _sigmoid(mg_ref[:, d:2 * d].astype(F32)) * _dot(og_ref[...], w1_ref[...])
    acc = acc + _sigmoid(mg_ref[:, 2 * d:3 * d].astype(F32)) * _dot(om_ref[...], w2_ref[...])
    o_ref[...] = acc.astype(BF16)


def _const_spec(shape):
    nd = len(shape)
    return pl.BlockSpec(shape, lambda i: (0,) * nd, pipeline_mode=pl.Buffered(1))


def _merge(o_f, o_b, proj, o_gqa, o_mem, g_gla, w0, w1, w2, lay, d, tm=512):
    t = proj.shape[0]
    ggb = lay["gg"] // GLA_VAL
    return pl.pallas_call(
        functools.partial(_merge_kernel, d=d),
        grid=(t // tm,),
        in_specs=[
            pl.BlockSpec((tm, GLA_VAL), lambda i: (i, 0)),
            pl.BlockSpec((tm, GLA_VAL), lambda i: (i, 0)),
            pl.BlockSpec((tm, GLA_VAL), lambda i: (i, ggb)),
            pl.BlockSpec((tm, GQA_Q), lambda i: (i, 0)),
            pl.BlockSpec((tm, MEM_W), lambda i: (i, 0)),
            pl.BlockSpec((tm, 3 * d), lambda i: (i, 0)),
            _const_spec((1, GLA_DV)),
            _const_spec((GLA_VAL, d)),
            _const_spec((GQA_Q, d)),
            _const_spec((MEM_W, d)),
        ],
        out_specs=pl.BlockSpec((tm, d), lambda i: (i, 0)),
        out_shape=jax.ShapeDtypeStruct((t, d), BF16),
        compiler_params=_cparams(("arbitrary",)),
        name="merge",
    )(o_f, o_b, proj, o_gqa, o_mem, proj, g_gla, w0, w1, w2)


ROUTER_ROWS = 40


def _outproj_kernel(mrg_ref, xp_ref, xs_ref, wo_ref, gf_ref, wrh_ref, wrl_ref, br_ref,
                    x1_ref, h2p_ref, lg_ref, *, npb, d):
    i = pl.program_id(0)
    x = jnp.where(i < npb, xp_ref[...], xs_ref[...])
    x1 = x + _dot(mrg_ref[...], wo_ref[...])
    x1_ref[...] = x1
    h2 = _rms(x1, gf_ref[...])
    h_hi = h2.astype(BF16)
    h_lo = (h2 - h_hi.astype(F32)).astype(BF16)
    lg = _dot_nt(wrh_ref[...], h_hi) + _dot_nt(wrh_ref[...], h_lo) + _dot_nt(wrl_ref[...], h_hi)
    lg_ref[...] = lg + br_ref[:, 0:1]
    for c in range(ROW_TILE):
        h2p_ref[_chunk_rows(h2.shape[0], c), :] = _pack_chunk(h2[:, c * PAIR_CHUNK:(c + 1) * PAIR_CHUNK])


def _outproj(merged, xp, xs, wo, gf, wr_hi, wr_lo, br, tm=512):
    tp, d = xp.shape
    t = merged.shape[0]
    npb = tp // tm
    assert d // 2 == ROW_TILE * LANES, "packed token rows must fill exactly one (8, 128) tile"
    return pl.pallas_call(
        functools.partial(_outproj_kernel, npb=npb, d=d),
        grid=(t // tm,),
        in_specs=[
            pl.BlockSpec((tm, d), lambda i: (i, 0)),
            pl.BlockSpec((tm, d), lambda i: (jnp.minimum(i, npb - 1), 0)),
            pl.BlockSpec((tm, d), lambda i: (jnp.maximum(i - npb, 0), 0)),
            _const_spec((d, d)),
            _const_spec((1, d)),
            _const_spec((ROUTER_ROWS, d)),
            _const_spec((ROUTER_ROWS, d)),
            _const_spec((ROUTER_ROWS, LANES)),
        ],
        out_specs=[
            pl.BlockSpec((tm, d), lambda i: (i, 0)),
            pl.BlockSpec((tm * ROW_TILE, LANES), lambda i: (i, 0)),
            pl.BlockSpec((ROUTER_ROWS, tm), lambda i: (0, i)),
        ],
        out_shape=[jax.ShapeDtypeStruct((t, d), F32), jax.ShapeDtypeStruct((t * ROW_TILE, LANES), jnp.uint32),
                   jax.ShapeDtypeStruct((ROUTER_ROWS, t), F32)],
        compiler_params=_cparams(("arbitrary",)),
        name="outproj",
    )(merged, xp, xs, wo, gf, wr_hi, wr_lo, br)


def _route_kernel(lg_ref, info_ref, w_ref, cnt_ref, carry_ref, *, tm):
    i = pl.program_id(0)

    @pl.when(i == 0)
    def _():
        carry_ref[...] = jnp.zeros_like(carry_ref)

    neg = -jnp.inf
    row8 = lax.broadcasted_iota(jnp.int32, (8, tm), 0)
    g = jnp.where(row8 < N_GROUPS, lg_ref[0:8, :], neg)
    gmax = jnp.max(g, axis=0, keepdims=True)
    gsum = jnp.sum(jnp.exp(g - gmax), axis=0, keepdims=True)
    p_grp = 1.0 / gsum
    grp = jnp.min(jnp.where(g == gmax, row8, 8), axis=0, keepdims=True)

    e8 = lg_ref[8:16, :]
    for gi in range(1, N_GROUPS):
        e8 = jnp.where(grp == gi, lg_ref[8 + 8 * gi:16 + 8 * gi, :], e8)
    m1 = jnp.max(e8, axis=0, keepdims=True)
    i1 = jnp.min(jnp.where(e8 == m1, row8, 8), axis=0, keepdims=True)
    e8b = jnp.where(row8 == i1, neg, e8)
    m2 = jnp.max(e8b, axis=0, keepdims=True)
    i2 = jnp.min(jnp.where(e8b == m2, row8, 8), axis=0, keepdims=True)
    zsum = jnp.sum(jnp.exp(e8 - m1), axis=0, keepdims=True)
    p1 = 1.0 / zsum
    p2 = jnp.exp(m2 - m1) / zsum
    den = p1 + p2
    w1 = p_grp * p1 / den
    w2 = p_grp * p2 / den
    eid1 = grp * EXPERTS_PER_GROUP + i1
    eid2 = grp * EXPERTS_PER_GROUP + i2

    rowe = lax.broadcasted_iota(jnp.int32, (N_EXPERTS, tm), 0)
    oh1 = (rowe == eid1)
    oh2 = (rowe == eid2)
    tr = lax.broadcasted_iota(jnp.int32, (tm, tm), 0)
    tc = lax.broadcasted_iota(jnp.int32, (tm, tm), 1)
    upper = (tr < tc).astype(BF16)
    cum1 = _dot(oh1.astype(BF16), upper)
    cum2 = _dot(oh2.astype(BF16), upper)
    oh1f = oh1.astype(F32)
    oh2f = oh2.astype(F32)
    tot1 = jnp.sum(oh1f, axis=1, keepdims=True)
    tot2 = jnp.sum(oh2f, axis=1, keepdims=True)
    carry = carry_ref[:, 0:1]
    rank1 = jnp.sum(oh1f * (cum1 + carry), axis=0, keepdims=True)
    rank2 = jnp.sum(oh2f * (cum2 + carry + tot1), axis=0, keepdims=True)
    new_carry = carry + tot1 + tot2
    carry_ref[...] = jnp.broadcast_to(new_carry, carry_ref.shape)
    cnt_ref[...] = jnp.broadcast_to(new_carry, cnt_ref.shape)

    zi = jnp.zeros((1, tm), jnp.int32)
    info_ref[...] = jnp.concatenate(
        [eid1, eid2, rank1.astype(jnp.int32), rank2.astype(jnp.int32), zi, zi, zi, zi], axis=0)
    zf = jnp.zeros((1, tm), F32)
    w_ref[...] = jnp.concatenate([w1, w2, zf, zf, zf, zf, zf, zf], axis=0)


def _route(logits_t, tm=512):
    t = logits_t.shape[1]
    return pl.pallas_call(
        functools.partial(_route_kernel, tm=tm),
        grid=(t // tm,),
        in_specs=[pl.BlockSpec((ROUTER_ROWS, tm), lambda i: (0, i))],
        out_specs=[
            pl.BlockSpec((8, tm), lambda i: (0, i)),
            pl.BlockSpec((8, tm), lambda i: (0, i)),
            pl.BlockSpec((N_EXPERTS, LANES), lambda i: (0, 0)),
        ],
        out_shape=[jax.ShapeDtypeStruct((8, t), jnp.int32), jax.ShapeDtypeStruct((8, t), F32),
                   jax.ShapeDtypeStruct((N_EXPERTS, LANES), F32)],
        scratch_shapes=[pltpu.VMEM((N_EXPERTS, LANES), F32)],
        compiler_params=_cparams(("arbitrary",)),
        name="route",
    )(logits_t)


MOE_BLK = 256


def _dispatch_kernel(zb_ref, dest_ref, h_ref, xs_ref, zero_ref, sem, zsem, *, tm, nb):
    blk_rows = MOE_BLK * ROW_TILE

    @pl.when(pl.program_id(0) == 0)
    def _():
        zero_ref[...] = jnp.zeros_like(zero_ref)

        def zero_copy(b):
            return pltpu.make_async_copy(zero_ref, xs_ref.at[pl.ds(b * blk_rows, blk_rows)], zsem)

        def zstart(b, carry):
            @pl.when(zb_ref[b] == 1)
            def _():
                zero_copy(b).start()
            return carry

        def zwait(b, carry):
            @pl.when(zb_ref[b] == 1)
            def _():
                zero_copy(b).wait()
            return carry

        lax.fori_loop(0, nb, zstart, 0)
        lax.fori_loop(0, nb, zwait, 0)

    def row_copy(t, d):
        src = h_ref.at[pl.ds(pl.multiple_of(t * ROW_TILE, ROW_TILE), ROW_TILE)]
        dst = xs_ref.at[pl.ds(pl.multiple_of(d * ROW_TILE, ROW_TILE), ROW_TILE)]
        return pltpu.make_async_copy(src, dst, sem)

    def issue(t, carry):
        row_copy(t, dest_ref[0, t]).start(priority=0)
        row_copy(t, dest_ref[1, t]).start(priority=1)
        return carry

    lax.fori_loop(0, tm, issue, 0, unroll=8)
    tile_wait = pltpu.make_async_copy(h_ref, xs_ref.at[pl.ds(0, tm * ROW_TILE)], sem)
    tile_wait.wait()
    tile_wait.wait()


def _dispatch(zero_blk, dest, h2p, nb, tm=512):
    t = h2p.shape[0] // ROW_TILE
    grid_spec = pltpu.PrefetchScalarGridSpec(
        num_scalar_prefetch=1,
        grid=(t // tm,),
        in_specs=[
            pl.BlockSpec((8, tm), lambda i, zb: (0, i), memory_space=pltpu.SMEM),
            pl.BlockSpec((tm * ROW_TILE, LANES), lambda i, zb: (i, 0)),
        ],
        out_specs=pl.BlockSpec(memory_space=pl.ANY),
        scratch_shapes=[pltpu.VMEM((MOE_BLK * ROW_TILE, LANES), jnp.uint32), pltpu.SemaphoreType.DMA(()),
                        pltpu.SemaphoreType.DMA(())],
    )
    return pl.pallas_call(
        functools.partial(_dispatch_kernel, tm=tm, nb=nb),
        grid_spec=grid_spec,
        out_shape=jax.ShapeDtypeStruct((nb * MOE_BLK * ROW_TILE, LANES), jnp.uint32),
        compiler_params=_cparams(("arbitrary",)),
        name="dispatch",
    )(zero_blk, dest, h2p)


def _expert_kernel(be_ref, nu_ref, nxt_ref, par_ref, x_ref, wg_hbm, wu_hbm, wd_hbm, y_ref,
                   wgf_ref, wuf_ref, wdf_ref, wgb_ref, wub_ref, wdb_ref, wsem, *, d):
    b = pl.program_id(0)

    def weight_copies(e, s):
        return (pltpu.make_async_copy(wg_hbm.at[e], wgf_ref.at[s], wsem.at[s]),
                pltpu.make_async_copy(wu_hbm.at[e], wuf_ref.at[s], wsem.at[s]),
                pltpu.make_async_copy(wd_hbm.at[e], wdf_ref.at[s], wsem.at[s]))

    @pl.when(b == 0)
    def _():
        for cp in weight_copies(be_ref[0], 0):
            cp.start()

    @pl.when(jnp.logical_or(b == 0, be_ref[b] != be_ref[jnp.maximum(b - 1, 0)]))
    def _():
        s = par_ref[b]
        for cp in weight_copies(be_ref[b], s):
            cp.wait()
        wgb_ref[...] = wgf_ref[s].astype(BF16)
        wub_ref[...] = wuf_ref[s].astype(BF16)
        wdb_ref[...] = wdf_ref[s].astype(BF16)

        @pl.when(nxt_ref[b] >= 0)
        def _():
            for cp in weight_copies(nxt_ref[b], 1 - s):
                cp.start()

    @pl.when(b < nu_ref[0])
    def _():
        gate = jnp.zeros((MOE_BLK, D_EXPERT), F32)
        up = jnp.zeros((MOE_BLK, D_EXPERT), F32)
        for c in range(ROW_TILE):
            xc = _unpack_chunk(x_ref[_chunk_rows(MOE_BLK, c), :]).astype(BF16)
            ks = slice(c * PAIR_CHUNK, (c + 1) * PAIR_CHUNK)
            gate = gate + _dot(xc, wgb_ref[ks, :])
            up = up + _dot(xc, wub_ref[ks, :])
        hid = (gate * _sigmoid(gate) * up).astype(BF16)
        for c in range(ROW_TILE):
            yc = _dot(hid, wdb_ref[:, c * PAIR_CHUNK:(c + 1) * PAIR_CHUNK])
            y_ref[_chunk_rows(MOE_BLK, c), :] = _pack_chunk(yc)

    @pl.when(b >= nu_ref[0])
    def _():
        y_ref[...] = jnp.zeros_like(y_ref)


def _experts(blk_expert, n_used, next_expert, parity, xs, wg, wu, wd, d):
    p = xs.shape[0] // ROW_TILE
    nb = p // MOE_BLK
    blk_rows = MOE_BLK * ROW_TILE
    grid_spec = pltpu.PrefetchScalarGridSpec(
        num_scalar_prefetch=4,
        grid=(nb,),
        in_specs=[
            pl.BlockSpec((blk_rows, LANES), lambda b, be, nu, nx, pa: (jnp.minimum(b, nu[0] - 1), 0)),
            pl.BlockSpec(memory_space=pl.ANY),
            pl.BlockSpec(memory_space=pl.ANY),
            pl.BlockSpec(memory_space=pl.ANY),
        ],
        out_specs=pl.BlockSpec((blk_rows, LANES), lambda b, be, nu, nx, pa: (b, 0)),
        scratch_shapes=[pltpu.VMEM((2, d, D_EXPERT), F32), pltpu.VMEM((2, d, D_EXPERT), F32),
                        pltpu.VMEM((2, D_EXPERT, d), F32),
                        pltpu.VMEM((d, D_EXPERT), BF16), pltpu.VMEM((d, D_EXPERT), BF16),
                        pltpu.VMEM((D_EXPERT, d), BF16), pltpu.SemaphoreType.DMA((2,))],
    )
    return pl.pallas_call(
        functools.partial(_expert_kernel, d=d),
        grid_spec=grid_spec,
        out_shape=jax.ShapeDtypeStruct((p * ROW_TILE, LANES), jnp.uint32),
        compiler_params=_cparams(("arbitrary",)),
        name="experts",
    )(blk_expert, n_used, next_expert, parity, xs, wg, wu, wd)


COMBINE_GROUP = 8


def _combine_kernel(dcur_ref, dnxt_ref, x1_ref, wt_ref, yb_ref, op_ref, os_ref, buf_ref, wrep_ref, sem,
                    *, tm, npb, nsteps):
    i = pl.program_id(0)
    slot = lax.rem(i, 2)
    nslot = 1 - slot
    grp = COMBINE_GROUP

    for k in range(TOP_K):
        wrep_ref[:, k * LANES:(k + 1) * LANES] = jnp.broadcast_to(wt_ref[:, k:k + 1], (tm, LANES))

    def row_copy(s, k, t, d):
        src = yb_ref.at[pl.ds(pl.multiple_of(d * ROW_TILE, ROW_TILE), ROW_TILE)]
        dst = buf_ref.at[s, k, pl.ds(pl.multiple_of(t * ROW_TILE, ROW_TILE), ROW_TILE)]
        return pltpu.make_async_copy(src, dst, sem.at[s])

    def issue_group(dref, s, t0):
        for u in range(grp):
            row_copy(s, 0, t0 + u, dref[0, t0 + u]).start(priority=0)
            row_copy(s, 1, t0 + u, dref[1, t0 + u]).start(priority=1)

    def wait_slot(s):
        for k in range(TOP_K):
            pltpu.make_async_copy(yb_ref.at[pl.ds(0, tm * ROW_TILE)], buf_ref.at[s, k], sem.at[s]).wait()

    @pl.when(i == 0)
    def _():
        def first(g, carry):
            issue_group(dcur_ref, 0, g * grp)
            return carry

        lax.fori_loop(0, tm // grp, first, 0)

    wait_slot(slot)

    def run(o_ref):
        def group(g, carry):
            t0 = pl.multiple_of(g * grp, grp)
            issue_group(dnxt_ref, nslot, t0)
            rows = pl.ds(t0, grp)
            w1 = wrep_ref[rows, 0:LANES]
            w2 = wrep_ref[rows, LANES:2 * LANES]
            for c in range(ROW_TILE):
                chunk = _chunk_rows(grp, c, t0)
                a_lo, a_hi = _unpack_pair(buf_ref[slot, 0, chunk, :])
                b_lo, b_hi = _unpack_pair(buf_ref[slot, 1, chunk, :])
                lo = slice(c * PAIR_CHUNK, c * PAIR_CHUNK + LANES)
                hi = slice(c * PAIR_CHUNK + LANES, (c + 1) * PAIR_CHUNK)
                o_ref[rows, lo] = x1_ref[rows, lo] + a_lo * w1 + b_lo * w2
                o_ref[rows, hi] = x1_ref[rows, hi] + a_hi * w1 + b_hi * w2
            return carry

        lax.fori_loop(0, tm // grp, group, 0)

    @pl.when(i < npb)
    def _():
        run(op_ref)

    @pl.when(i >= npb)
    def _():
        run(os_ref)

    @pl.when(i == nsteps - 1)
    def _():
        wait_slot(nslot)


def _combine(dest, x1, wt, yb, tp, tm=256):
    t, d = x1.shape
    npb = tp // tm
    nsteps = t // tm
    return pl.pallas_call(
        functools.partial(_combine_kernel, tm=tm, npb=npb, nsteps=nsteps),
        grid=(nsteps,),
        in_specs=[
            pl.BlockSpec((8, tm), lambda i: (0, i), memory_space=pltpu.SMEM),
            pl.BlockSpec((8, tm), lambda i: (0, jnp.minimum(i + 1, nsteps - 1)), memory_space=pltpu.SMEM),
            pl.BlockSpec((tm, d), lambda i: (i, 0)),
            pl.BlockSpec((tm, 8), lambda i: (i, 0)),
            pl.BlockSpec(memory_space=pl.ANY),
        ],
        out_specs=[
            pl.BlockSpec((tm, d), lambda i: (jnp.minimum(i, npb - 1), 0)),
            pl.BlockSpec((tm, d), lambda i: (jnp.maximum(i - npb, 0), 0)),
        ],
        out_shape=[jax.ShapeDtypeStruct((tp, d), F32), jax.ShapeDtypeStruct((t - tp, d), F32)],
        scratch_shapes=[pltpu.VMEM((2, TOP_K, tm * ROW_TILE, LANES), jnp.uint32),
                        pltpu.VMEM((tm, TOP_K * LANES), F32), pltpu.SemaphoreType.DMA((2,))],
        compiler_params=_cparams(("arbitrary",)),
        name="combine",
    )(dest, dest, x1, wt, yb)


def _rope_tables(length):
    rows = length // GRID_W
    row = jnp.broadcast_to(jnp.arange(rows, dtype=F32)[:, None], (rows, GRID_W)).reshape(length)
    col = jnp.broadcast_to(jnp.arange(GRID_W, dtype=F32)[None, :], (rows, GRID_W)).reshape(length)
    axis_dim = HEAD_DIM // 2
    freqs = ROPE_THETA ** (-jnp.arange(0, axis_dim, 2, dtype=F32) / axis_dim)
    ang = jnp.concatenate([row[:, None] * freqs, col[:, None] * freqs], axis=-1)
    c, s = jnp.cos(ang), jnp.sin(ang)
    return jnp.concatenate([c, c], axis=1), jnp.concatenate([-s, s], axis=1)


def _layer(xp, xs, mem_all, seqs, n_mem, p):
    d = xp.shape[1]
    tp = xp.shape[0]
    lay = _proj_layout(d)

    w_in = p["w_in"]
    sizes = (GLA_KEY, GLA_KEY, GLA_VAL, GLA_VAL, GLA_RANK, GLA_RANK, GQA_Q, GQA_KV, GQA_KV, MEM_W, 3 * d)
    offs = np.concatenate([[0], np.cumsum(sizes)])
    seg = {n: w_in[:, offs[k]:offs[k + 1]] for k, n in enumerate(
        ("gq", "gk", "gv", "gg", "ga_f", "ga_b", "aq", "ak", "av", "mq", "mg"))}
    def deinterleave(w, heads):
        rows = w.shape[0]
        return w.reshape(rows, heads, HEAD_DIM // 2, 2).transpose(0, 1, 3, 2).reshape(rows, heads * HEAD_DIM)

    w_main = jnp.concatenate([seg["mg"], seg["gv"], seg["gg"], deinterleave(seg["aq"], GQA_HEADS), seg["mq"],
                              seg["gq"], seg["gk"], deinterleave(seg["ak"], GQA_KV_HEADS), seg["av"]],
                             axis=1).astype(BF16)
    w_ga = jnp.concatenate([seg["ga_f"], seg["ga_b"], jnp.zeros((d, LANES - 2 * GLA_RANK), F32)],
                           axis=1).astype(BF16)

    def a2_pad(a2, row0):
        pad = [jnp.zeros((row0, GLA_KEY), F32)] if row0 else []
        pad += [a2, jnp.zeros((LANES - row0 - GLA_RANK, GLA_KEY), F32)]
        return jnp.concatenate(pad, axis=0).astype(BF16)

    gq_perm = deinterleave(p["g_q_gqa"][None, :], 1)
    gk_perm = deinterleave(p["g_k_gqa"][None, :], 1)

    proj, ga = _inproj(xp, xs, p["g_mix"][None, :], w_main, w_ga)

    o_f, o_b = _gla(proj, ga, a2_pad(p["gla_a2_fwd"], 0), p["gla_ab_fwd"][None, :],
                    a2_pad(p["gla_a2_bwd"], GLA_RANK), p["gla_ab_bwd"][None, :], lay, seqs)

    max_len = max(length for _, length in seqs)
    cos_t, sin_t = _rope_tables(max_len)
    qt3, ka, vt3 = _qkprep(proj, cos_t, sin_t, gq_perm, gk_perm, lay, seqs)
    groups = []
    for (start, length) in seqs:
        if groups and groups[-1][2] == length and groups[-1][0] + groups[-1][1] * length == start:
            groups[-1][1] += 1
        else:
            groups.append([start, 1, length])
    o_gqa = jnp.zeros((proj.shape[0], GQA_Q), BF16)
    for (start, nseq, length) in groups:
        o_gqa = _flash_group(qt3, ka, vt3, start, nseq, length, o_gqa)

    km, vm = _memkv(mem_all, p["g_mem_norm"][None, :], p["w_mem_kv"].astype(BF16), p["g_k_mem"][None, :])
    o_mem = _memattn(proj, km, vm, p["g_q_mem"][None, :], lay, seqs, n_mem)

    merged = _merge(o_f, o_b, proj, o_gqa, o_mem, p["g_gla_out"][None, :], p["w_br_gla"].astype(BF16),
                    p["w_br_gqa"].astype(BF16), p["w_br_mem"].astype(BF16), lay, d)
    gpad = 8 - N_GROUPS
    wr = jnp.concatenate([p["w_router_group"].T, jnp.zeros((gpad, d), F32), p["w_router_expert"].T], axis=0)
    wr_hi = wr.astype(BF16)
    wr_lo = (wr - wr_hi.astype(F32)).astype(BF16)
    br = jnp.concatenate([p["b_router_group"], jnp.zeros((gpad,), F32), p["b_router_expert"]])
    br = jnp.broadcast_to(br[:, None], (ROUTER_ROWS, LANES))
    x1, h2p, logits_t = _outproj(merged, xp, xs, p["w_out"].astype(BF16), p["g_ffn"][None, :], wr_hi, wr_lo, br)

    info, wts, counts = _route(logits_t)
    t = x1.shape[0]
    cnt = counts[:, 0].astype(jnp.int32)
    padded = ((cnt + MOE_BLK - 1) // MOE_BLK) * MOE_BLK
    pad_ends = jnp.cumsum(padded)
    pad_starts = pad_ends - padded
    nb = (t * TOP_K) // MOE_BLK + N_EXPERTS
    n_used = (pad_ends[-1] // MOE_BLK).astype(jnp.int32)
    blk = jnp.arange(nb, dtype=jnp.int32)
    first_row = jnp.minimum(blk, n_used - 1) * MOE_BLK
    blk_expert = jnp.sum((pad_ends[None, :] <= first_row[:, None]).astype(jnp.int32), axis=1)
    blk_expert = jnp.minimum(blk_expert, N_EXPERTS - 1)
    ex = jnp.arange(N_EXPERTS, dtype=jnp.int32)
    owns = cnt > 0
    later = jnp.logical_and(ex[None, :] > ex[:, None], owns[None, :])
    next_of = jnp.min(jnp.where(later, ex[None, :], N_EXPERTS), axis=1)
    next_of = jnp.where(next_of == N_EXPERTS, -1, next_of)
    slot_of = (jnp.cumsum(owns.astype(jnp.int32)) - 1) % 2
    onehot_be = blk_expert[:, None] == ex[None, :]
    next_expert = jnp.sum(jnp.where(onehot_be, next_of[None, :], 0), axis=1)
    parity = jnp.sum(jnp.where(onehot_be, slot_of[None, :], 0), axis=1)
    ragged = jnp.logical_and(cnt > 0, cnt % MOE_BLK != 0)
    is_last = jnp.any(jnp.logical_and((pad_ends // MOE_BLK - 1)[None, :] == blk[:, None], ragged[None, :]), axis=1)
    zero_blk = jnp.logical_or(is_last, blk >= n_used).astype(jnp.int32)
    eid = info[0:2]
    sel = eid[None] == jnp.arange(N_EXPERTS, dtype=jnp.int32)[:, None, None]
    dest = info[2:4] + jnp.sum(jnp.where(sel, pad_starts[:, None, None], 0), axis=0)
    dest = jnp.concatenate([dest, jnp.zeros((6, t), jnp.int32)], axis=0)

    xs_sorted = _dispatch(zero_blk, dest, h2p, nb)
    yb = _experts(blk_expert, n_used[None], next_expert, parity, xs_sorted,
                  p["w_exp_gate"], p["w_exp_up"], p["w_exp_down"], d)
    return _combine(dest, x1, wts.T, yb, tp)


def kernel(x_prompt, x_sample, mem_prompt, mem_sample, g_mix, w_in, gla_a2_fwd, gla_ab_fwd, gla_a2_bwd, gla_ab_bwd, g_gla_out, g_q_gqa, g_k_gqa, g_mem_norm, w_mem_kv, g_q_mem, g_k_mem, w_br_gla, w_br_gqa, w_br_mem, w_out, g_ffn, w_router_group, b_router_group, w_router_expert, b_router_expert, w_exp_gate, w_exp_up, w_exp_down):
    bp, lp, d = x_prompt.shape
    bs, ls, _ = x_sample.shape
    n_mem = mem_prompt.shape[1]
    depth = g_mix.shape[0]
    seqs = [(b * lp, lp) for b in range(bp)] + [(bp * lp + b * ls, ls) for b in range(bs)]
    xp = x_prompt.reshape(bp * lp, d)
    xs = x_sample.reshape(bs * ls, d)
    mem_all = jnp.concatenate([mem_prompt.reshape(bp * n_mem, d), mem_sample.reshape(bs * n_mem, d)], axis=0)
    names = ("g_mix", "w_in", "gla_a2_fwd", "gla_ab_fwd", "gla_a2_bwd", "gla_ab_bwd", "g_gla_out", "g_q_gqa",
             "g_k_gqa", "g_mem_norm", "w_mem_kv", "g_q_mem", "g_k_mem", "w_br_gla", "w_br_gqa", "w_br_mem",
             "w_out", "g_ffn", "w_router_group", "b_router_group", "w_router_expert", "b_router_expert",
             "w_exp_gate", "w_exp_up", "w_exp_down")
    vals = (g_mix, w_in, gla_a2_fwd, gla_ab_fwd, gla_a2_bwd, gla_ab_bwd, g_gla_out, g_q_gqa, g_k_gqa,
            g_mem_norm, w_mem_kv, g_q_mem, g_k_mem, w_br_gla, w_br_gqa, w_br_mem, w_out, g_ffn,
            w_router_group, b_router_group, w_router_expert, b_router_expert, w_exp_gate, w_exp_up, w_exp_down)
    for layer in range(depth):
        p = {n: v[layer] for n, v in zip(names, vals)}
        xp, xs = _layer(xp, xs, mem_all, seqs, n_mem, p)
    return xp.reshape(bp, lp, d), xs.reshape(bs, ls, d)
```

```python
import functools

import numpy as np
import jax
import jax.numpy as jnp
from jax import lax
from jax.experimental import pallas as pl
from jax.experimental.pallas import tpu as pltpu

F32 = jnp.float32
BF16 = jnp.bfloat16

GRID_W = 64
GLA_HEADS, GLA_DK, GLA_DV = 4, 128, 256
GLA_KEY, GLA_VAL = GLA_HEADS * GLA_DK, GLA_HEADS * GLA_DV
GLA_RANK, GLA_TAU, GLA_CHUNK = 16, 16.0, 64
GQA_HEADS, GQA_KV_HEADS, HEAD_DIM = 8, 2, 128
GQA_GROUP = GQA_HEADS // GQA_KV_HEADS
GQA_Q, GQA_KV = GQA_HEADS * HEAD_DIM, GQA_KV_HEADS * HEAD_DIM
ROPE_THETA = 10000.0
MEM_HEADS, MEM_HD = 4, 256
MEM_W = MEM_HEADS * MEM_HD
N_GROUPS, EXPERTS_PER_GROUP, TOP_K, D_EXPERT = 4, 8, 2, 512
N_EXPERTS = N_GROUPS * EXPERTS_PER_GROUP
EPS = 1e-6

LANES = 128
VMEM_LIMIT_BYTES = 56 * 1024 * 1024


def _proj_layout(d):
    off = {}
    c = 0
    for name, w in (("mg", 3 * d), ("gv", GLA_VAL), ("gg", GLA_VAL), ("aq", GQA_Q), ("mq", MEM_W),
                    ("gq", GLA_KEY), ("gk", GLA_KEY), ("ak", GQA_KV), ("av", GQA_KV)):
        off[name] = c
        c += w
    off["total"] = c
    return off


def _cparams(sem, vmem=VMEM_LIMIT_BYTES):
    return pltpu.CompilerParams(dimension_semantics=sem, vmem_limit_bytes=vmem)


def _rms(x, g):
    ms = jnp.mean(x * x, axis=-1, keepdims=True)
    return x * lax.rsqrt(ms + EPS) * g


def _dot(a, b):
    return jnp.dot(a, b, preferred_element_type=F32)


def _dot_nt(a, b):
    return lax.dot_general(a, b, (((1,), (1,)), ((), ())), preferred_element_type=F32)


def _dot_tn(a, b):
    return lax.dot_general(a, b, (((0,), (0,)), ((), ())), preferred_element_type=F32)


def _sigmoid(x):
    return 1.0 / (1.0 + jnp.exp(-x))


def _pack_pair(lo, hi):
    lo_b = pltpu.bitcast(lo.astype(BF16).astype(F32), jnp.uint32)
    hi_b = pltpu.bitcast(hi.astype(BF16).astype(F32), jnp.uint32)
    return (hi_b & jnp.uint32(0xFFFF0000)) | (lo_b >> 16)


def _unpack_pair(w):
    lo = pltpu.bitcast(w << 16, F32)
    hi = pltpu.bitcast(w & jnp.uint32(0xFFFF0000), F32)
    return lo, hi


ROW_TILE = 8
PAIR_CHUNK = 2 * LANES


def _chunk_rows(n, chunk, start=0):
    return pl.ds(start * ROW_TILE + chunk, n, stride=ROW_TILE)


def _pack_chunk(x):
    return _pack_pair(x[:, :LANES], x[:, LANES:])


def _unpack_chunk(w):
    lo, hi = _unpack_pair(w)
    return jnp.concatenate([lo, hi], axis=1)


def _inproj_kernel(xp_ref, xs_ref, g_ref, w_ref, wga_ref, o_ref, ga_ref, h_ref, *, npb):
    i = pl.program_id(0)
    j = pl.program_id(1)

    def norm(x_ref):
        hb = _rms(x_ref[...], g_ref[...]).astype(BF16)
        h_ref[...] = hb
        ga_ref[...] = _dot(hb, wga_ref[...])

    @pl.when(jnp.logical_and(j == 0, i < npb))
    def _():
        norm(xp_ref)

    @pl.when(jnp.logical_and(j == 0, i >= npb))
    def _():
        norm(xs_ref)

    o_ref[...] = _dot(h_ref[...], w_ref[...]).astype(BF16)


def _inproj(xp, xs, g, w_main, w_ga, tm=1024, tn=512):
    tp, d = xp.shape
    ts = xs.shape[0]
    t = tp + ts
    nc = w_main.shape[1]
    npb = tp // tm
    grid = (t // tm, nc // tn)
    return pl.pallas_call(
        functools.partial(_inproj_kernel, npb=npb),
        grid=grid,
        in_specs=[
            pl.BlockSpec((tm, d), lambda i, j: (jnp.minimum(i, npb - 1), 0)),
            pl.BlockSpec((tm, d), lambda i, j: (jnp.maximum(i - npb, 0), 0)),
            pl.BlockSpec((1, d), lambda i, j: (0, 0)),
            pl.BlockSpec((d, tn), lambda i, j: (0, j)),
            pl.BlockSpec((d, LANES), lambda i, j: (0, 0)),
        ],
        out_specs=[
            pl.BlockSpec((tm, tn), lambda i, j: (i, j)),
            pl.BlockSpec((tm, LANES), lambda i, j: (i, 0)),
        ],
        out_shape=[jax.ShapeDtypeStruct((t, nc), BF16), jax.ShapeDtypeStruct((t, LANES), F32)],
        scratch_shapes=[pltpu.VMEM((tm, d), BF16)],
        compiler_params=_cparams(("arbitrary", "arbitrary")),
        name="inproj",
    )(xp, xs, g, w_main, w_ga)


GLA_TB = 256


def _gla_kernel(rbf_ref, rbb_ref, first_ref,
                qf_ref, kf_ref, vf_ref, gaf_ref, qb_ref, kb_ref, vb_ref, gab_ref,
                a2f_ref, abf_ref, a2b_ref, abb_ref, of_ref, ob_ref, sf_ref, sb_ref):
    @pl.when(first_ref[pl.program_id(0)] == 1)
    def _():
        sf_ref[...] = jnp.zeros_like(sf_ref)
        sb_ref[...] = jnp.zeros_like(sb_ref)

    _gla_direction(qf_ref, kf_ref, vf_ref, gaf_ref, a2f_ref, abf_ref, of_ref, sf_ref, False)
    _gla_direction(qb_ref, kb_ref, vb_ref, gab_ref, a2b_ref, abb_ref, ob_ref, sb_ref, True)


def _gla_direction(q_ref, k_ref, v_ref, ga_ref, a2_ref, ab_ref, o_ref, s_ref, reverse):
    tb, c = GLA_TB, GLA_CHUNK
    nch = tb // c

    r = lax.broadcasted_iota(jnp.int32, (tb, tb), 0)
    cc = lax.broadcasted_iota(jnp.int32, (tb, tb), 1)
    same = (r >> 6) == (cc >> 6)
    if reverse:
        tri = (cc >= r).astype(BF16)
        amask = jnp.logical_and(same, cc > r)
    else:
        tri = (cc <= r).astype(BF16)
        amask = jnp.logical_and(same, cc <= r)

    z = _dot(ga_ref[...].astype(BF16), a2_ref[...]) + ab_ref[...]
    la = (jnp.minimum(z, 0.0) - jnp.log(1.0 + jnp.exp(-jnp.abs(z)))) * (1.0 / GLA_TAU)
    la_hi = la.astype(BF16)
    la_lo = (la - la_hi.astype(F32)).astype(BF16)
    pre_all = _dot(tri, la_hi) + _dot(tri, la_lo)
    scale = GLA_DK ** -0.5
    for h in range(GLA_HEADS):
        pre = pre_all[:, h * GLA_DK:(h + 1) * GLA_DK]
        bs, tots = [], []
        for ci in range(nch):
            r0 = ci * c
            if reverse:
                base = pre[r0 + c:r0 + c + 1] if ci < nch - 1 else jnp.zeros((1, GLA_DK), F32)
                tot = pre[r0:r0 + 1] - base
            else:
                base = pre[r0 - 1:r0] if ci > 0 else jnp.zeros((1, GLA_DK), F32)
                tot = pre[r0 + c - 1:r0 + c] - base
            bs.append(pre[r0:r0 + c] - base)
            tots.append(tot)
        b = jnp.concatenate(bs, axis=0)
        btot = jnp.concatenate([jnp.broadcast_to(tt, (c, GLA_DK)) for tt in tots], axis=0)

        q = q_ref[:, h * GLA_DK:(h + 1) * GLA_DK].astype(F32) * scale
        k = k_ref[:, h * GLA_DK:(h + 1) * GLA_DK].astype(F32)
        v = v_ref[:, h * GLA_DV:(h + 1) * GLA_DV]
        q_e = (q * jnp.exp(b)).astype(BF16)
        k_e = (k * jnp.exp(-b)).astype(BF16)
        k_d = (k * jnp.exp(btot - b)).astype(BF16)
        att = jnp.where(amask, _dot_nt(q_e, k_e), 0.0).astype(BF16)
        o_intra = _dot(att, v)

        st = s_ref[h]
        outs = [None] * nch
        order = range(nch - 1, -1, -1) if reverse else range(nch)
        for ci in order:
            r0 = ci * c
            o_inter = _dot_nt(q_e[r0:r0 + c], st.astype(BF16))
            outs[ci] = o_intra[r0:r0 + c] + o_inter
            kvt = _dot_tn(v[r0:r0 + c], k_d[r0:r0 + c])
            st = st * jnp.exp(tots[ci]) + kvt
        s_ref[h] = st
        o_ref[:, h * GLA_DV:(h + 1) * GLA_DV] = jnp.concatenate(outs, axis=0).astype(BF16)


def _gla(proj, ga, a2f, abf, a2b, abb, lay, seqs):
    t = proj.shape[0]
    tb = GLA_TB
    rbf, rbb, first = [], [], []
    for (start, length) in seqs:
        blocks = list(range(start // tb, (start + length) // tb))
        rbf += blocks
        rbb += blocks[::-1]
        first += [1] + [0] * (len(blocks) - 1)
    rbf, rbb, first = (jnp.asarray(np.array(a, np.int32)) for a in (rbf, rbb, first))
    qb, kb, vb = lay["gq"] // GLA_KEY, lay["gk"] // GLA_KEY, lay["gv"] // GLA_VAL

    def data_specs(pick):
        return [
            pl.BlockSpec((tb, GLA_KEY), lambda j, rf, rb, fr: (pick(rf, rb)[j], qb)),
            pl.BlockSpec((tb, GLA_KEY), lambda j, rf, rb, fr: (pick(rf, rb)[j], kb)),
            pl.BlockSpec((tb, GLA_VAL), lambda j, rf, rb, fr: (pick(rf, rb)[j], vb)),
            pl.BlockSpec((tb, LANES), lambda j, rf, rb, fr: (pick(rf, rb)[j], 0)),
        ]

    def const2(shape):
        return pl.BlockSpec(shape, lambda j, rf, rb, fr: (0, 0))

    fwd = lambda rf, rb: rf
    bwd = lambda rf, rb: rb
    grid_spec = pltpu.PrefetchScalarGridSpec(
        num_scalar_prefetch=3,
        grid=(t // tb,),
        in_specs=data_specs(fwd) + data_specs(bwd) + [
            const2((LANES, GLA_KEY)), const2((1, GLA_KEY)), const2((LANES, GLA_KEY)), const2((1, GLA_KEY))],
        out_specs=[pl.BlockSpec((tb, GLA_VAL), lambda j, rf, rb, fr: (rf[j], 0)),
                   pl.BlockSpec((tb, GLA_VAL), lambda j, rf, rb, fr: (rb[j], 0))],
        scratch_shapes=[pltpu.VMEM((GLA_HEADS, GLA_DV, GLA_DK), F32),
                        pltpu.VMEM((GLA_HEADS, GLA_DV, GLA_DK), F32)],
    )
    return pl.pallas_call(
        _gla_kernel,
        grid_spec=grid_spec,
        out_shape=[jax.ShapeDtypeStruct((t, GLA_VAL), BF16), jax.ShapeDtypeStruct((t, GLA_VAL), BF16)],
        compiler_params=_cparams(("arbitrary",)),
        name="gla",
    )(rbf, rbb, first, proj, proj, proj, ga, proj, proj, proj, ga, a2f, abf, a2b, abb)


FLASH_TQ = 512
FLASH_TK = 512
LOG2E = 1.4426950408889634
BF16_SUBLANES = 16
VT_ROWS = HEAD_DIM + BF16_SUBLANES


def _qkprep_kernel(pb_ref, aq_ref, ak_ref, av_ref, cos_ref, sin_ref, gq_ref, gk_ref, qt_ref, ka_ref, vt_ref):
    cosv = cos_ref[...]
    sinv = sin_ref[...]
    scale = HEAD_DIM ** -0.5 * LOG2E
    nsub = FLASH_TK // FLASH_TQ

    def one(x, g):
        xn = _rms(x.astype(F32), g)
        return xn * cosv + pltpu.roll(xn, HEAD_DIM // 2, 1) * sinv

    for h in range(GQA_HEADS):
        sl = slice(h * HEAD_DIM, (h + 1) * HEAD_DIM)
        qt = (one(aq_ref[:, sl], gq_ref[...]) * scale).T.astype(BF16)
        for u in range(nsub):
            qt_ref[u, sl, :] = qt[:, u * FLASH_TQ:(u + 1) * FLASH_TQ]
    for h in range(GQA_KV_HEADS):
        sl = slice(h * HEAD_DIM, (h + 1) * HEAD_DIM)
        ka_ref[:, sl] = one(ak_ref[:, sl], gk_ref[...]).astype(BF16)
        vt_ref[0, h * VT_ROWS:h * VT_ROWS + HEAD_DIM, :] = av_ref[:, sl].astype(F32).T.astype(BF16)
        vt_ref[0, h * VT_ROWS + HEAD_DIM:(h + 1) * VT_ROWS, :] = jnp.ones((BF16_SUBLANES, FLASH_TK), BF16)


def _qkprep(proj, cos_t, sin_t, gq, gk, lay, seqs):
    t = proj.shape[0]
    tm = FLASH_TK
    nsub = FLASH_TK // FLASH_TQ
    pb = []
    for (start, length) in seqs:
        pb += list(range(length // tm))
    pb = jnp.asarray(np.array(pb, np.int32))
    aqb, akb = lay["aq"] // GQA_Q, lay["ak"] // GQA_KV
    grid_spec = pltpu.PrefetchScalarGridSpec(
        num_scalar_prefetch=1,
        grid=(t // tm,),
        in_specs=[
            pl.BlockSpec((tm, GQA_Q), lambda i, p: (i, aqb)),
            pl.BlockSpec((tm, GQA_KV), lambda i, p: (i, akb)),
            pl.BlockSpec((tm, GQA_KV), lambda i, p: (i, akb + 1)),
            pl.BlockSpec((tm, HEAD_DIM), lambda i, p: (p[i], 0)),
            pl.BlockSpec((tm, HEAD_DIM), lambda i, p: (p[i], 0)),
            pl.BlockSpec((1, HEAD_DIM), lambda i, p: (0, 0)),
            pl.BlockSpec((1, HEAD_DIM), lambda i, p: (0, 0)),
        ],
        out_specs=[
            pl.BlockSpec((nsub, GQA_Q, FLASH_TQ), lambda i, p: (i, 0, 0)),
            pl.BlockSpec((tm, GQA_KV), lambda i, p: (i, 0)),
            pl.BlockSpec((1, GQA_KV_HEADS * VT_ROWS, tm), lambda i, p: (i, 0, 0)),
        ],
    )
    return pl.pallas_call(
        _qkprep_kernel,
        grid_spec=grid_spec,
        out_shape=[jax.ShapeDtypeStruct((t // FLASH_TQ, GQA_Q, FLASH_TQ), BF16),
                   jax.ShapeDtypeStruct((t, GQA_KV), BF16),
                   jax.ShapeDtypeStruct((t // tm, GQA_KV_HEADS * VT_ROWS, tm), BF16)],
        compiler_params=_cparams(("arbitrary",)),
        name="qkprep",
    )(pb, proj, proj, proj, cos_t, sin_t, gq, gk)


FLASH_REFRAME = 64.0
FLASH_KEY_BLOCKS = 2


def _flash_kernel(q_ref, k_ref, v_ref, prev_ref, o_ref, acc_ref, p0_ref, p1_ref, frame_ref, mcur_ref, *, nk, kb):
    del prev_ref
    tq, tk = FLASH_TQ, kb * FLASH_TK
    qt = jnp.concatenate([q_ref[0, g * HEAD_DIM:(g + 1) * HEAD_DIM, :] for g in range(GQA_GROUP)], axis=1)

    def scores(ci):
        off = pl.multiple_of(ci * tk, tk)
        return _dot(k_ref[pl.ds(off, tk), :], qt)

    def pv(ci, pt):
        out = _dot(v_ref[ci * kb], pt[0:FLASH_TK])
        for u in range(1, kb):
            out = out + _dot(v_ref[ci * kb + u], pt[u * FLASH_TK:(u + 1) * FLASH_TK])
        return out

    def reframe(ci):
        c_new = jnp.maximum(frame_ref[...], mcur_ref[...])
        alpha = jnp.exp2(frame_ref[...] - c_new)
        pt = jnp.exp2(scores(ci) - c_new).astype(BF16)
        acc_ref[...] = alpha * acc_ref[...] + pv(ci, pt)
        frame_ref[...] = c_new

    st0 = scores(0)
    frame_ref[...] = jnp.max(st0, axis=0, keepdims=True)
    acc_ref[...] = pv(0, jnp.exp2(st0 - frame_ref[...]).astype(BF16))
    p_refs = (p0_ref, p1_ref)
    p0_ref[...] = jnp.zeros(p0_ref.shape, BF16)

    def step(ci, par, need_prev):
        cur_ref, prv_ref = p_refs[par], p_refs[1 - par]

        @pl.when(need_prev)
        def _():
            reframe(ci - 1)
            prv_ref[...] = jnp.zeros(prv_ref.shape, BF16)

        st = scores(ci)
        m_cur = jnp.max(st, axis=0, keepdims=True)
        mcur_ref[...] = m_cur
        cur_ref[...] = jnp.exp2(st - frame_ref[...]).astype(BF16)
        acc_ref[...] += pv(ci - 1, prv_ref[...])
        return jnp.max(m_cur - frame_ref[...]) > FLASH_REFRAME

    def pair(j, need_prev):
        ci = 1 + 2 * j
        return step(ci + 1, 0, step(ci, 1, need_prev))

    need_last = lax.fori_loop(0, (nk - 1) // 2, pair, False)
    if (nk - 1) % 2:
        need_last = step(nk - 1, (nk - 1) % 2, need_last)
    last = nk - 1

    @pl.when(need_last)
    def _():
        reframe(last)

    @pl.when(jnp.logical_not(need_last))
    def _():
        acc_ref[...] += pv(last, p_refs[last % 2][...])

    out_t = acc_ref[0:HEAD_DIM, :] / acc_ref[HEAD_DIM:HEAD_DIM + 1, :]
    for g in range(GQA_GROUP):
        o_ref[:, g * HEAD_DIM:(g + 1) * HEAD_DIM] = out_t[:, g * tq:(g + 1) * tq].T.astype(BF16)


def _flash_group(qt3, ka, vt3, start, nseq, length, prev):
    t = ka.shape[0]
    tq, tk = FLASH_TQ, FLASH_TK
    kb = FLASH_KEY_BLOCKS if (length // tk) % FLASH_KEY_BLOCKS == 0 and length // tk > 2 * FLASH_KEY_BLOCKS else 1
    nqt = length // tq
    nk = length // tk
    qb0 = start // tq
    sb0 = start // length
    gw = GQA_GROUP * HEAD_DIM
    in_specs = [
        pl.BlockSpec((1, gw, tq), lambda s, kv, i: (qb0 + s * nqt + i, kv, 0)),
        pl.BlockSpec((length, HEAD_DIM), lambda s, kv, i: (sb0 + s, kv)),
        pl.BlockSpec((nk, VT_ROWS, tk), lambda s, kv, i: (sb0 + s, kv, 0)),
        pl.BlockSpec(memory_space=pl.ANY),
    ]
    return pl.pallas_call(
        functools.partial(_flash_kernel, nk=nk // kb, kb=kb),
        grid=(nseq, GQA_KV_HEADS, nqt),
        in_specs=in_specs,
        out_specs=pl.BlockSpec((tq, gw), lambda s, kv, i: (qb0 + s * nqt + i, kv)),
        out_shape=jax.ShapeDtypeStruct((t, GQA_Q), BF16),
        scratch_shapes=[pltpu.VMEM((VT_ROWS, GQA_GROUP * tq), F32),
                        pltpu.VMEM((kb * tk, GQA_GROUP * tq), BF16),
                        pltpu.VMEM((kb * tk, GQA_GROUP * tq), BF16),
                        pltpu.VMEM((1, GQA_GROUP * tq), F32),
                        pltpu.VMEM((1, GQA_GROUP * tq), F32)],
        input_output_aliases={3: 0},
        compiler_params=_cparams(("arbitrary", "arbitrary", "arbitrary")),
        name="gqa_flash",
    )(qt3, ka, vt3, prev)


def _memkv_kernel(mem_ref, g_ref, w_ref, gk_ref, km_ref, vm_ref):
    hb = _rms(mem_ref[...], g_ref[...]).astype(BF16)
    kv = _dot(hb, w_ref[...])
    for h in range(MEM_HEADS):
        sl = slice(h * MEM_HD, (h + 1) * MEM_HD)
        km_ref[:, sl] = _rms(kv[:, sl], gk_ref[...]).astype(BF16)
    vm_ref[...] = kv[:, MEM_W:].astype(BF16)


def _memkv(mem, g, w, gk):
    rows, d = mem.shape
    m = 256
    return pl.pallas_call(
        _memkv_kernel,
        grid=(rows // m,),
        in_specs=[
            pl.BlockSpec((m, d), lambda i: (i, 0)),
            pl.BlockSpec((1, d), lambda i: (0, 0)),
            pl.BlockSpec((d, 2 * MEM_W), lambda i: (0, 0)),
            pl.BlockSpec((1, MEM_HD), lambda i: (0, 0)),
        ],
        out_specs=[pl.BlockSpec((m, MEM_W), lambda i: (i, 0)), pl.BlockSpec((m, MEM_W), lambda i: (i, 0))],
        out_shape=[jax.ShapeDtypeStruct((rows, MEM_W), BF16), jax.ShapeDtypeStruct((rows, MEM_W), BF16)],
        compiler_params=_cparams(("arbitrary",)),
        name="memkv",
    )(mem, g, w, gk)


def _memattn_kernel(sq_ref, mq_ref, km_ref, vm_ref, gq_ref, o_ref):
    scale = MEM_HD ** -0.5
    for h in range(MEM_HEADS):
        sl = slice(h * MEM_HD, (h + 1) * MEM_HD)
        qn = (_rms(mq_ref[:, sl].astype(F32), gq_ref[...]) * scale).astype(BF16)
        s = _dot_nt(qn, km_ref[:, sl])
        m = jnp.max(s, axis=1, keepdims=True)
        p = jnp.exp(s - m)
        l = jnp.sum(p, axis=1, keepdims=True)
        o_ref[:, sl] = (_dot(p.astype(BF16), vm_ref[:, sl]) / l).astype(BF16)


def _memattn(proj, km, vm, gq, lay, seqs, n_mem, tm=512):
    t = proj.shape[0]
    sq = []
    for si, (start, length) in enumerate(seqs):
        sq += [si] * (length // tm)
    sq = jnp.asarray(np.array(sq, np.int32))
    mqb = lay["mq"] // MEM_W
    grid_spec = pltpu.PrefetchScalarGridSpec(
        num_scalar_prefetch=1,
        grid=(t // tm,),
        in_specs=[
            pl.BlockSpec((tm, MEM_W), lambda i, s: (i, mqb)),
            pl.BlockSpec((n_mem, MEM_W), lambda i, s: (s[i], 0)),
            pl.BlockSpec((n_mem, MEM_W), lambda i, s: (s[i], 0)),
            pl.BlockSpec((1, MEM_HD), lambda i, s: (0, 0)),
        ],
        out_specs=pl.BlockSpec((tm, MEM_W), lambda i, s: (i, 0)),
    )
    return pl.pallas_call(
        _memattn_kernel,
        grid_spec=grid_spec,
        out_shape=jax.ShapeDtypeStruct((t, MEM_W), BF16),
        compiler_params=_cparams(("arbitrary",)),
        name="memattn",
    )(sq, proj, km, vm, gq)


def _merge_kernel(of_ref, ob_ref, gg_ref, og_ref, om_ref, mg_ref, ggla_ref, w0_ref, w1_ref, w2_ref, o_ref, *, d):
    o = of_ref[...].astype(F32) + ob_ref[...].astype(F32)
    gg = gg_ref[...].astype(F32)
    parts = []
    for h in range(GLA_HEADS):
        sl = slice(h * GLA_DV, (h + 1) * GLA_DV)
        gh = gg[:, sl]
        parts.append((_rms(o[:, sl], ggla_ref[...]) * (gh * _sigmoid(gh))).astype(BF16))
    y0 = jnp.concatenate(parts, axis=1)
    acc = _sigmoid(mg_ref[:, 0:d].astype(F32)) * _dot(y0, w0_ref[...])
    acc = acc + _sigmoid(mg_ref[:, d:2 * d].astype(F32)) * _dot(og_ref[...], w1_ref[...])
    acc = acc + _sigmoid(mg_ref[:, 2 * d:3 * d].astype(F32)) * _dot(om_ref[...], w2_ref[...])
    o_ref[...] = acc.astype(BF16)


def _const_spec(shape):
    nd = len(shape)
    return pl.BlockSpec(shape, lambda i: (0,) * nd, pipeline_mode=pl.Buffered(1))


def _merge(o_f, o_b, proj, o_gqa, o_mem, g_gla, w0, w1, w2, lay, d, tm=512):
    t = proj.shape[0]
    ggb = lay["gg"] // GLA_VAL
    return pl.pallas_call(
        functools.partial(_merge_kernel, d=d),
        grid=(t // tm,),
        in_specs=[
            pl.BlockSpec((tm, GLA_VAL), lambda i: (i, 0)),
            pl.BlockSpec((tm, GLA_VAL), lambda i: (i, 0)),
            pl.BlockSpec((tm, GLA_VAL), lambda i: (i, ggb)),
            pl.BlockSpec((tm, GQA_Q), lambda i: (i, 0)),
            pl.BlockSpec((tm, MEM_W), lambda i: (i, 0)),
            pl.BlockSpec((tm, 3 * d), lambda i: (i, 0)),
            _const_spec((1, GLA_DV)),
            _const_spec((GLA_VAL, d)),
            _const_spec((GQA_Q, d)),
            _const_spec((MEM_W, d)),
        ],
        out_specs=pl.BlockSpec((tm, d), lambda i: (i, 0)),
        out_shape=jax.ShapeDtypeStruct((t, d), BF16),
        compiler_params=_cparams(("arbitrary",)),
        name="merge",
    )(o_f, o_b, proj, o_gqa, o_mem, proj, g_gla, w0, w1, w2)


ROUTER_ROWS = 40


def _outproj_kernel(mrg_ref, xp_ref, xs_ref, wo_ref, gf_ref, wrh_ref, wrl_ref, br_ref,
                    x1_ref, h2p_ref, lg_ref, *, npb, d):
    i = pl.program_id(0)
    x = jnp.where(i < npb, xp_ref[...], xs_ref[...])
    x1 = x + _dot(mrg_ref[...], wo_ref[...])
    x1_ref[...] = x1
    h2 = _rms(x1, gf_ref[...])
    h_hi = h2.astype(BF16)
    h_lo = (h2 - h_hi.astype(F32)).astype(BF16)
    lg = _dot_nt(wrh_ref[...], h_hi) + _dot_nt(wrh_ref[...], h_lo) + _dot_nt(wrl_ref[...], h_hi)
    lg_ref[...] = lg + br_ref[:, 0:1]
    for c in range(ROW_TILE):
        h2p_ref[_chunk_rows(h2.shape[0], c), :] = _pack_chunk(h2[:, c * PAIR_CHUNK:(c + 1) * PAIR_CHUNK])


def _outproj(merged, xp, xs, wo, gf, wr_hi, wr_lo, br, tm=512):
    tp, d = xp.shape
    t = merged.shape[0]
    npb = tp // tm
    assert d // 2 == ROW_TILE * LANES, "packed token rows must fill exactly one (8, 128) tile"
    return pl.pallas_call(
        functools.partial(_outproj_kernel, npb=npb, d=d),
        grid=(t // tm,),
        in_specs=[
            pl.BlockSpec((tm, d), lambda i: (i, 0)),
            pl.BlockSpec((tm, d), lambda i: (jnp.minimum(i, npb - 1), 0)),
            pl.BlockSpec((tm, d), lambda i: (jnp.maximum(i - npb, 0), 0)),
            _const_spec((d, d)),
            _const_spec((1, d)),
            _const_spec((ROUTER_ROWS, d)),
            _const_spec((ROUTER_ROWS, d)),
            _const_spec((ROUTER_ROWS, LANES)),
        ],
        out_specs=[
            pl.BlockSpec((tm, d), lambda i: (i, 0)),
            pl.BlockSpec((tm * ROW_TILE, LANES), lambda i: (i, 0)),
            pl.BlockSpec((ROUTER_ROWS, tm), lambda i: (0, i)),
        ],
        out_shape=[jax.ShapeDtypeStruct((t, d), F32), jax.ShapeDtypeStruct((t * ROW_TILE, LANES), jnp.uint32),
                   jax.ShapeDtypeStruct((ROUTER_ROWS, t), F32)],
        compiler_params=_cparams(("arbitrary",)),
        name="outproj",
    )(merged, xp, xs, wo, gf, wr_hi, wr_lo, br)


def _route_kernel(lg_ref, info_ref, w_ref, cnt_ref, carry_ref, *, tm):
    i = pl.program_id(0)

    @pl.when(i == 0)
    def _():
        carry_ref[...] = jnp.zeros_like(carry_ref)

    neg = -jnp.inf
    row8 = lax.broadcasted_iota(jnp.int32, (8, tm), 0)
    g = jnp.where(row8 < N_GROUPS, lg_ref[0:8, :], neg)
    gmax = jnp.max(g, axis=0, keepdims=True)
    gsum = jnp.sum(jnp.exp(g - gmax), axis=0, keepdims=True)
    p_grp = 1.0 / gsum
    grp = jnp.min(jnp.where(g == gmax, row8, 8), axis=0, keepdims=True)

    e8 = lg_ref[8:16, :]
    for gi in range(1, N_GROUPS):
        e8 = jnp.where(grp == gi, lg_ref[8 + 8 * gi:16 + 8 * gi, :], e8)
    m1 = jnp.max(e8, axis=0, keepdims=True)
    i1 = jnp.min(jnp.where(e8 == m1, row8, 8), axis=0, keepdims=True)
    e8b = jnp.where(row8 == i1, neg, e8)
    m2 = jnp.max(e8b, axis=0, keepdims=True)
    i2 = jnp.min(jnp.where(e8b == m2, row8, 8), axis=0, keepdims=True)
    zsum = jnp.sum(jnp.exp(e8 - m1), axis=0, keepdims=True)
    p1 = 1.0 / zsum
    p2 = jnp.exp(m2 - m1) / zsum
    den = p1 + p2
    w1 = p_grp * p1 / den
    w2 = p_grp * p2 / den
    eid1 = grp * EXPERTS_PER_GROUP + i1
    eid2 = grp * EXPERTS_PER_GROUP + i2

    rowe = lax.broadcasted_iota(jnp.int32, (N_EXPERTS, tm), 0)
    oh1 = (rowe == eid1)
    oh2 = (rowe == eid2)
    tr = lax.broadcasted_iota(jnp.int32, (tm, tm), 0)
    tc = lax.broadcasted_iota(jnp.int32, (tm, tm), 1)
    upper = (tr < tc).astype(BF16)
    cum1 = _dot(oh1.astype(BF16), upper)
    cum2 = _dot(oh2.astype(BF16), upper)
    oh1f = oh1.astype(F32)
    oh2f = oh2.astype(F32)
    tot1 = jnp.sum(oh1f, axis=1, keepdims=True)
    tot2 = jnp.sum(oh2f, axis=1, keepdims=True)
    carry = carry_ref[:, 0:1]
    rank1 = jnp.sum(oh1f * (cum1 + carry), axis=0, keepdims=True)
    rank2 = jnp.sum(oh2f * (cum2 + carry + tot1), axis=0, keepdims=True)
    new_carry = carry + tot1 + tot2
    carry_ref[...] = jnp.broadcast_to(new_carry, carry_ref.shape)
    cnt_ref[...] = jnp.broadcast_to(new_carry, cnt_ref.shape)

    zi = jnp.zeros((1, tm), jnp.int32)
    info_ref[...] = jnp.concatenate(
        [eid1, eid2, rank1.astype(jnp.int32), rank2.astype(jnp.int32), zi, zi, zi, zi], axis=0)
    zf = jnp.zeros((1, tm), F32)
    w_ref[...] = jnp.concatenate([w1, w2, zf, zf, zf, zf, zf, zf], axis=0)


def _route(logits_t, tm=512):
    t = logits_t.shape[1]
    return pl.pallas_call(
        functools.partial(_route_kernel, tm=tm),
        grid=(t // tm,),
        in_specs=[pl.BlockSpec((ROUTER_ROWS, tm), lambda i: (0, i))],
        out_specs=[
            pl.BlockSpec((8, tm), lambda i: (0, i)),
            pl.BlockSpec((8, tm), lambda i: (0, i)),
            pl.BlockSpec((N_EXPERTS, LANES), lambda i: (0, 0)),
        ],
        out_shape=[jax.ShapeDtypeStruct((8, t), jnp.int32), jax.ShapeDtypeStruct((8, t), F32),
                   jax.ShapeDtypeStruct((N_EXPERTS, LANES), F32)],
        scratch_shapes=[pltpu.VMEM((N_EXPERTS, LANES), F32)],
        compiler_params=_cparams(("arbitrary",)),
        name="route",
    )(logits_t)


MOE_BLK = 256


def _dispatch_kernel(zb_ref, dest_ref, h_ref, xs_ref, zero_ref, sem, zsem, *, tm, nb):
    blk_rows = MOE_BLK * ROW_TILE

    @pl.when(pl.program_id(0) == 0)
    def _():
        zero_ref[...] = jnp.zeros_like(zero_ref)

        def zero_copy(b):
            return pltpu.make_async_copy(zero_ref, xs_ref.at[pl.ds(b * blk_rows, blk_rows)], zsem)

        def zstart(b, carry):
            @pl.when(zb_ref[b] == 1)
            def _():
                zero_copy(b).start()
            return carry

        def zwait(b, carry):
            @pl.when(zb_ref[b] == 1)
            def _():
                zero_copy(b).wait()
            return carry

        lax.fori_loop(0, nb, zstart, 0)
        lax.fori_loop(0, nb, zwait, 0)

    def row_copy(t, d):
        src = h_ref.at[pl.ds(pl.multiple_of(t * ROW_TILE, ROW_TILE), ROW_TILE)]
        dst = xs_ref.at[pl.ds(pl.multiple_of(d * ROW_TILE, ROW_TILE), ROW_TILE)]
        return pltpu.make_async_copy(src, dst, sem)

    def issue(t, carry):
        row_copy(t, dest_ref[0, t]).start(priority=0)
        row_copy(t, dest_ref[1, t]).start(priority=1)
        return carry

    lax.fori_loop(0, tm, issue, 0, unroll=8)
    tile_wait = pltpu.make_async_copy(h_ref, xs_ref.at[pl.ds(0, tm * ROW_TILE)], sem)
    tile_wait.wait()
    tile_wait.wait()


def _dispatch(zero_blk, dest, h2p, nb, tm=512):
    t = h2p.shape[0] // ROW_TILE
    grid_spec = pltpu.PrefetchScalarGridSpec(
        num_scalar_prefetch=1,
        grid=(t // tm,),
        in_specs=[
            pl.BlockSpec((8, tm), lambda i, zb: (0, i), memory_space=pltpu.SMEM),
            pl.BlockSpec((tm * ROW_TILE, LANES), lambda i, zb: (i, 0)),
        ],
        out_specs=pl.BlockSpec(memory_space=pl.ANY),
        scratch_shapes=[pltpu.VMEM((MOE_BLK * ROW_TILE, LANES), jnp.uint32), pltpu.SemaphoreType.DMA(()),
                        pltpu.SemaphoreType.DMA(())],
    )
    return pl.pallas_call(
        functools.partial(_dispatch_kernel, tm=tm, nb=nb),
        grid_spec=grid_spec,
        out_shape=jax.ShapeDtypeStruct((nb * MOE_BLK * ROW_TILE, LANES), jnp.uint32),
        compiler_params=_cparams(("arbitrary",)),
        name="dispatch",
    )(zero_blk, dest, h2p)


def _expert_kernel(be_ref, nu_ref, nxt_ref, par_ref, x_ref, wg_hbm, wu_hbm, wd_hbm, y_ref,
                   wgf_ref, wuf_ref, wdf_ref, wgb_ref, wub_ref, wdb_ref, wsem, *, d):
    b = pl.program_id(0)

    def weight_copies(e, s):
        return (pltpu.make_async_copy(wg_hbm.at[e], wgf_ref.at[s], wsem.at[s]),
                pltpu.make_async_copy(wu_hbm.at[e], wuf_ref.at[s], wsem.at[s]),
                pltpu.make_async_copy(wd_hbm.at[e], wdf_ref.at[s], wsem.at[s]))

    @pl.when(b == 0)
    def _():
        for cp in weight_copies(be_ref[0], 0):
            cp.start()

    @pl.when(jnp.logical_or(b == 0, be_ref[b] != be_ref[jnp.maximum(b - 1, 0)]))
    def _():
        s = par_ref[b]
        for cp in weight_copies(be_ref[b], s):
            cp.wait()
        wgb_ref[...] = wgf_ref[s].astype(BF16)
        wub_ref[...] = wuf_ref[s].astype(BF16)
        wdb_ref[...] = wdf_ref[s].astype(BF16)

        @pl.when(nxt_ref[b] >= 0)
        def _():
            for cp in weight_copies(nxt_ref[b], 1 - s):
                cp.start()

    @pl.when(b < nu_ref[0])
    def _():
        gate = jnp.zeros((MOE_BLK, D_EXPERT), F32)
        up = jnp.zeros((MOE_BLK, D_EXPERT), F32)
        for c in range(ROW_TILE):
            xc = _unpack_chunk(x_ref[_chunk_rows(MOE_BLK, c), :]).astype(BF16)
            ks = slice(c * PAIR_CHUNK, (c + 1) * PAIR_CHUNK)
            gate = gate + _dot(xc, wgb_ref[ks, :])
            up = up + _dot(xc, wub_ref[ks, :])
        hid = (gate * _sigmoid(gate) * up).astype(BF16)
        for c in range(ROW_TILE):
            yc = _dot(hid, wdb_ref[:, c * PAIR_CHUNK:(c + 1) * PAIR_CHUNK])
            y_ref[_chunk_rows(MOE_BLK, c), :] = _pack_chunk(yc)

    @pl.when(b >= nu_ref[0])
    def _():
        y_ref[...] = jnp.zeros_like(y_ref)


def _experts(blk_expert, n_used, next_expert, parity, xs, wg, wu, wd, d):
    p = xs.shape[0] // ROW_TILE
    nb = p // MOE_BLK
    blk_rows = MOE_BLK * ROW_TILE
    grid_spec = pltpu.PrefetchScalarGridSpec(
        num_scalar_prefetch=4,
        grid=(nb,),
        in_specs=[
            pl.BlockSpec((blk_rows, LANES), lambda b, be, nu, nx, pa: (jnp.minimum(b, nu[0] - 1), 0)),
            pl.BlockSpec(memory_space=pl.ANY),
            pl.BlockSpec(memory_space=pl.ANY),
            pl.BlockSpec(memory_space=pl.ANY),
        ],
        out_specs=pl.BlockSpec((blk_rows, LANES), lambda b, be, nu, nx, pa: (b, 0)),
        scratch_shapes=[pltpu.VMEM((2, d, D_EXPERT), F32), pltpu.VMEM((2, d, D_EXPERT), F32),
                        pltpu.VMEM((2, D_EXPERT, d), F32),
                        pltpu.VMEM((d, D_EXPERT), BF16), pltpu.VMEM((d, D_EXPERT), BF16),
                        pltpu.VMEM((D_EXPERT, d), BF16), pltpu.SemaphoreType.DMA((2,))],
    )
    return pl.pallas_call(
        functools.partial(_expert_kernel, d=d),
        grid_spec=grid_spec,
        out_shape=jax.ShapeDtypeStruct((p * ROW_TILE, LANES), jnp.uint32),
        compiler_params=_cparams(("arbitrary",)),
        name="experts",
    )(blk_expert, n_used, next_expert, parity, xs, wg, wu, wd)


COMBINE_GROUP = 8


def _combine_kernel(dcur_ref, dnxt_ref, x1_ref, wt_ref, yb_ref, op_ref, os_ref, buf_ref, wrep_ref, sem,
                    *, tm, npb, nsteps):
    i = pl.program_id(0)
    slot = lax.rem(i, 2)
    nslot = 1 - slot
    grp = COMBINE_GROUP

    for k in range(TOP_K):
        wrep_ref[:, k * LANES:(k + 1) * LANES] = jnp.broadcast_to(wt_ref[:, k:k + 1], (tm, LANES))

    def row_copy(s, k, t, d):
        src = yb_ref.at[pl.ds(pl.multiple_of(d * ROW_TILE, ROW_TILE), ROW_TILE)]
        dst = buf_ref.at[s, k, pl.ds(pl.multiple_of(t * ROW_TILE, ROW_TILE), ROW_TILE)]
        return pltpu.make_async_copy(src, dst, sem.at[s])

    def issue_group(dref, s, t0):
        for u in range(grp):
            row_copy(s, 0, t0 + u, dref[0, t0 + u]).start(priority=0)
            row_copy(s, 1, t0 + u, dref[1, t0 + u]).start(priority=1)

    def wait_slot(s):
        for k in range(TOP_K):
            pltpu.make_async_copy(yb_ref.at[pl.ds(0, tm * ROW_TILE)], buf_ref.at[s, k], sem.at[s]).wait()

    @pl.when(i == 0)
    def _():
        def first(g, carry):
            issue_group(dcur_ref, 0, g * grp)
            return carry

        lax.fori_loop(0, tm // grp, first, 0)

    wait_slot(slot)

    def run(o_ref):
        def group(g, carry):
            t0 = pl.multiple_of(g * grp, grp)
            issue_group(dnxt_ref, nslot, t0)
            rows = pl.ds(t0, grp)
            w1 = wrep_ref[rows, 0:LANES]
            w2 = wrep_ref[rows, LANES:2 * LANES]
            for c in range(ROW_TILE):
                chunk = _chunk_rows(grp, c, t0)
                a_lo, a_hi = _unpack_pair(buf_ref[slot, 0, chunk, :])
                b_lo, b_hi = _unpack_pair(buf_ref[slot, 1, chunk, :])
                lo = slice(c * PAIR_CHUNK, c * PAIR_CHUNK + LANES)
                hi = slice(c * PAIR_CHUNK + LANES, (c + 1) * PAIR_CHUNK)
                o_ref[rows, lo] = x1_ref[rows, lo] + a_lo * w1 + b_lo * w2
                o_ref[rows, hi] = x1_ref[rows, hi] + a_hi * w1 + b_hi * w2
            return carry

        lax.fori_loop(0, tm // grp, group, 0)

    @pl.when(i < npb)
    def _():
        run(op_ref)

    @pl.when(i >= npb)
    def _():
        run(os_ref)

    @pl.when(i == nsteps - 1)
    def _():
        wait_slot(nslot)


def _combine(dest, x1, wt, yb, tp, tm=256):
    t, d = x1.shape
    npb = tp // tm
    nsteps = t // tm
    return pl.pallas_call(
        functools.partial(_combine_kernel, tm=tm, npb=npb, nsteps=nsteps),
        grid=(nsteps,),
        in_specs=[
            pl.BlockSpec((8, tm), lambda i: (0, i), memory_space=pltpu.SMEM),
            pl.BlockSpec((8, tm), lambda i: (0, jnp.minimum(i + 1, nsteps - 1)), memory_space=pltpu.SMEM),
            pl.BlockSpec((tm, d), lambda i: (i, 0)),
            pl.BlockSpec((tm, 8), lambda i: (i, 0)),
            pl.BlockSpec(memory_space=pl.ANY),
        ],
        out_specs=[
            pl.BlockSpec((tm, d), lambda i: (jnp.minimum(i, npb - 1), 0)),
            pl.BlockSpec((tm, d), lambda i: (jnp.maximum(i - npb, 0), 0)),
        ],
        out_shape=[jax.ShapeDtypeStruct((tp, d), F32), jax.ShapeDtypeStruct((t - tp, d), F32)],
        scratch_shapes=[pltpu.VMEM((2, TOP_K, tm * ROW_TILE, LANES), jnp.uint32),
                        pltpu.VMEM((tm, TOP_K * LANES), F32), pltpu.SemaphoreType.DMA((2,))],
        compiler_params=_cparams(("arbitrary",)),
        name="combine",
    )(dest, dest, x1, wt, yb)


def _rope_tables(length):
    rows = length // GRID_W
    row = jnp.broadcast_to(jnp.arange(rows, dtype=F32)[:, None], (rows, GRID_W)).reshape(length)
    col = jnp.broadcast_to(jnp.arange(GRID_W, dtype=F32)[None, :], (rows, GRID_W)).reshape(length)
    axis_dim = HEAD_DIM // 2
    freqs = ROPE_THETA ** (-jnp.arange(0, axis_dim, 2, dtype=F32) / axis_dim)
    ang = jnp.concatenate([row[:, None] * freqs, col[:, None] * freqs], axis=-1)
    c, s = jnp.cos(ang), jnp.sin(ang)
    return jnp.concatenate([c, c], axis=1), jnp.concatenate([-s, s], axis=1)


def _layer(xp, xs, mem_all, seqs, n_mem, p):
    d = xp.shape[1]
    tp = xp.shape[0]
    lay = _proj_layout(d)

    w_in = p["w_in"]
    sizes = (GLA_KEY, GLA_KEY, GLA_VAL, GLA_VAL, GLA_RANK, GLA_RANK, GQA_Q, GQA_KV, GQA_KV, MEM_W, 3 * d)
    offs = np.concatenate([[0], np.cumsum(sizes)])
    seg = {n: w_in[:, offs[k]:offs[k + 1]] for k, n in enumerate(
        ("gq", "gk", "gv", "gg", "ga_f", "ga_b", "aq", "ak", "av", "mq", "mg"))}
    def deinterleave(w, heads):
        rows = w.shape[0]
        return w.reshape(rows, heads, HEAD_DIM // 2, 2).transpose(0, 1, 3, 2).reshape(rows, heads * HEAD_DIM)

    w_main = jnp.concatenate([seg["mg"], seg["gv"], seg["gg"], deinterleave(seg["aq"], GQA_HEADS), seg["mq"],
                              seg["gq"], seg["gk"], deinterleave(seg["ak"], GQA_KV_HEADS), seg["av"]],
                             axis=1).astype(BF16)
    w_ga = jnp.concatenate([seg["ga_f"], seg["ga_b"], jnp.zeros((d, LANES - 2 * GLA_RANK), F32)],
                           axis=1).astype(BF16)

    def a2_pad(a2, row0):
        pad = [jnp.zeros((row0, GLA_KEY), F32)] if row0 else []
        pad += [a2, jnp.zeros((LANES - row0 - GLA_RANK, GLA_KEY), F32)]
        return jnp.concatenate(pad, axis=0).astype(BF16)

    gq_perm = deinterleave(p["g_q_gqa"][None, :], 1)
    gk_perm = deinterleave(p["g_k_gqa"][None, :], 1)

    proj, ga = _inproj(xp, xs, p["g_mix"][None, :], w_main, w_ga)

    o_f, o_b = _gla(proj, ga, a2_pad(p["gla_a2_fwd"], 0), p["gla_ab_fwd"][None, :],
                    a2_pad(p["gla_a2_bwd"], GLA_RANK), p["gla_ab_bwd"][None, :], lay, seqs)

    max_len = max(length for _, length in seqs)
    cos_t, sin_t = _rope_tables(max_len)
    qt3, ka, vt3 = _qkprep(proj, cos_t, sin_t, gq_perm, gk_perm, lay, seqs)
    groups = []
    for (start, length) in seqs:
        if groups and groups[-1][2] == length and groups[-1][0] + groups[-1][1] * length == start:
            groups[-1][1] += 1
        else:
            groups.append([start, 1, length])
    o_gqa = jnp.zeros((proj.shape[0], GQA_Q), BF16)
    for (start, nseq, length) in groups:
        o_gqa = _flash_group(qt3, ka, vt3, start, nseq, length, o_gqa)

    km, vm = _memkv(mem_all, p["g_mem_norm"][None, :], p["w_mem_kv"].astype(BF16), p["g_k_mem"][None, :])
    o_mem = _memattn(proj, km, vm, p["g_q_mem"][None, :], lay, seqs, n_mem)

    merged = _merge(o_f, o_b, proj, o_gqa, o_mem, p["g_gla_out"][None, :], p["w_br_gla"].astype(BF16),
                    p["w_br_gqa"].astype(BF16), p["w_br_mem"].astype(BF16), lay, d)
    gpad = 8 - N_GROUPS
    wr = jnp.concatenate([p["w_router_group"].T, jnp.zeros((gpad, d), F32), p["w_router_expert"].T], axis=0)
    wr_hi = wr.astype(BF16)
    wr_lo = (wr - wr_hi.astype(F32)).astype(BF16)
    br = jnp.concatenate([p["b_router_group"], jnp.zeros((gpad,), F32), p["b_router_expert"]])
    br = jnp.broadcast_to(br[:, None], (ROUTER_ROWS, LANES))
    x1, h2p, logits_t = _outproj(merged, xp, xs, p["w_out"].astype(BF16), p["g_ffn"][None, :], wr_hi, wr_lo, br)

    info, wts, counts = _route(logits_t)
    t = x1.shape[0]
    cnt = counts[:, 0].astype(jnp.int32)
    padded = ((cnt + MOE_BLK - 1) // MOE_BLK) * MOE_BLK
    pad_ends = jnp.cumsum(padded)
    pad_starts = pad_ends - padded
    nb = (t * TOP_K) // MOE_BLK + N_EXPERTS
    n_used = (pad_ends[-1] // MOE_BLK).astype(jnp.int32)
    blk = jnp.arange(nb, dtype=jnp.int32)
    first_row = jnp.minimum(blk, n_used - 1) * MOE_BLK
    blk_expert = jnp.sum((pad_ends[None, :] <= first_row[:, None]).astype(jnp.int32), axis=1)
    blk_expert = jnp.minimum(blk_expert, N_EXPERTS - 1)
    ex = jnp.arange(N_EXPERTS, dtype=jnp.int32)
    owns = cnt > 0
    later = jnp.logical_and(ex[None, :] > ex[:, None], owns[None, :])
    next_of = jnp.min(jnp.where(later, ex[None, :], N_EXPERTS), axis=1)
    next_of = jnp.where(next_of == N_EXPERTS, -1, next_of)
    slot_of = (jnp.cumsum(owns.astype(jnp.int32)) - 1) % 2
    onehot_be = blk_expert[:, None] == ex[None, :]
    next_expert = jnp.sum(jnp.where(onehot_be, next_of[None, :], 0), axis=1)
    parity = jnp.sum(jnp.where(onehot_be, slot_of[None, :], 0), axis=1)
    ragged = jnp.logical_and(cnt > 0, cnt % MOE_BLK != 0)
    is_last = jnp.any(jnp.logical_and((pad_ends // MOE_BLK - 1)[None, :] == blk[:, None], ragged[None, :]), axis=1)
    zero_blk = jnp.logical_or(is_last, blk >= n_used).astype(jnp.int32)
    eid = info[0:2]
    sel = eid[None] == jnp.arange(N_EXPERTS, dtype=jnp.int32)[:, None, None]
    dest = info[2:4] + jnp.sum(jnp.where(sel, pad_starts[:, None, None], 0), axis=0)
    dest = jnp.concatenate([dest, jnp.zeros((6, t), jnp.int32)], axis=0)

    xs_sorted = _dispatch(zero_blk, dest, h2p, nb)
    yb = _experts(blk_expert, n_used[None], next_expert, parity, xs_sorted,
                  p["w_exp_gate"], p["w_exp_up"], p["w_exp_down"], d)
    return _combine(dest, x1, wts.T, yb, tp)


def kernel(x_prompt, x_sample, mem_prompt, mem_sample, g_mix, w_in, gla_a2_fwd, gla_ab_fwd, gla_a2_bwd, gla_ab_bwd, g_gla_out, g_q_gqa, g_k_gqa, g_mem_norm, w_mem_kv, g_q_mem, g_k_mem, w_br_gla, w_br_gqa, w_br_mem, w_out, g_ffn, w_router_group, b_router_group, w_router_expert, b_router_expert, w_exp_gate, w_exp_up, w_exp_down):
    bp, lp, d = x_prompt.shape
    bs, ls, _ = x_sample.shape
    n_mem = mem_prompt.shape[1]
    depth = g_mix.shape[0]
    seqs = [(b * lp, lp) for b in range(bp)] + [(bp * lp + b * ls, ls) for b in range(bs)]
    xp = x_prompt.reshape(bp * lp, d)
    xs = x_sample.reshape(bs * ls, d)
    mem_all = jnp.concatenate([mem_prompt.reshape(bp * n_mem, d), mem_sample.reshape(bs * n_mem, d)], axis=0)
    names = ("g_mix", "w_in", "gla_a2_fwd", "gla_ab_fwd", "gla_a2_bwd", "gla_ab_bwd", "g_gla_out", "g_q_gqa",
             "g_k_gqa", "g_mem_norm", "w_mem_kv", "g_q_mem", "g_k_mem", "w_br_gla", "w_br_gqa", "w_br_mem",
             "w_out", "g_ffn", "w_router_group", "b_router_group", "w_router_expert", "b_router_expert",
             "w_exp_gate", "w_exp_up", "w_exp_down")
    vals = (g_mix, w_in, gla_a2_fwd, gla_ab_fwd, gla_a2_bwd, gla_ab_bwd, g_gla_out, g_q_gqa, g_k_gqa,
            g_mem_norm, w_mem_kv, g_q_mem, g_k_mem, w_br_gla, w_br_gqa, w_br_mem, w_out, g_ffn,
            w_router_group, b_router_group, w_router_expert, b_router_expert, w_exp_gate, w_exp_up, w_exp_down)
    for layer in range(depth):
        p = {n: v[layer] for n, v in zip(names, vals)}
        xp, xs = _layer(xp, xs, mem_all, seqs, n_mem, p)
    return xp.reshape(bp, lp, d), xs.reshape(bs, ls, d)
```
